```python
import math
import jax, jax.numpy as jnp
from jax import lax
import numpy as np

D_MODEL = 1024
BATCH = 16
SEQ = 2048
DEPTH = 2
DEC_BATCH = 32
DEC_SEQ = 4
PAST_LEN = 16384
PAGE_SIZE = 128

D_MIX = D_MODEL
CONV_WIDTH = D_MIX // 2
CONV_K = 3
N_HEADS = 4
HEAD_DIM = 64
V_DIM = 2 * HEAD_DIM
QK_WIDTH = N_HEADS * 2 * HEAD_DIM
ATTN_WIDTH = N_HEADS * V_DIM
ROT_DIM = HEAD_DIM // 4
ROPE_THETA = 500000.0
D_FF = ((8 * D_MODEL // 3 + 127) // 128) * 128
N_IN = 3 * CONV_WIDTH + 2 * QK_WIDTH + ATTN_WIDTH
Q_BLOCK = 128
LN_EPS = 1e-5
RMS_EPS = 1e-5
N_SUB = 3
DEEPNORM_ALPHA = (2.0 * DEPTH) ** 0.25
DEEPNORM_BETA = (8.0 * DEPTH) ** -0.25

kernel_name = "hymba_conv_diffattn_macaron_deepnorm_adaln_step"


def layer_norm(x, g, b):
    xf = x.astype(jnp.float32)
    mu = jnp.mean(xf, axis=-1, keepdims=True)
    var = jnp.mean(jnp.square(xf - mu), axis=-1, keepdims=True)
    return ((xf - mu) * lax.rsqrt(var + LN_EPS) * g.astype(jnp.float32) + b.astype(jnp.float32)).astype(x.dtype)


def rms_norm(x, g):
    xf = x.astype(jnp.float32)
    return xf * lax.rsqrt(jnp.mean(xf * xf, axis=-1, keepdims=True) + RMS_EPS) * g.astype(jnp.float32)


def rope_partial(x, pos):
    half = ROT_DIM // 2
    inv_freq = jnp.power(ROPE_THETA, -jnp.arange(0, ROT_DIM, 2, dtype=jnp.float32) / ROT_DIM)
    ang = pos[:, None] * inv_freq[None, :]
    cos = jnp.cos(ang)[None, :, None, None, :]
    sin = jnp.sin(ang)[None, :, None, None, :]
    xf = x.astype(jnp.float32)
    x1 = xf[..., :half]
    x2 = xf[..., half:ROT_DIM]
    out = jnp.concatenate([x1 * cos - x2 * sin, x2 * cos + x1 * sin, xf[..., ROT_DIM:]], axis=-1)
    return out.astype(x.dtype)


def swiglu(h, w_up, w_down):
    g, u = jnp.split(h @ w_up, 2, axis=-1)
    return (jax.nn.silu(g) * u) @ w_down


def diff_attention(q, k, v, q_pos, k_pos, lam):
    s = jnp.einsum('bqhcd,bkhcd->bhcqk', q.astype(jnp.float32), k.astype(jnp.float32)) * (HEAD_DIM ** -0.5)
    mask = k_pos[None, :] <= q_pos[:, None]
    s = jnp.where(mask, s, -jnp.inf)
    p = jax.nn.softmax(s, axis=-1)
    a = p[:, :, 0] - lam * p[:, :, 1]
    return jnp.einsum('bhqk,bkhe->bqhe', a, v.astype(jnp.float32))


def token_mixer(h, conv_prefix, past_k, past_v, pos0, w_in, w_conv, lam_qk, subln_g, w_o, lambda_init):
    b, s, _ = h.shape
    proj = h @ w_in
    splits = [CONV_WIDTH, 2 * CONV_WIDTH, 3 * CONV_WIDTH, 3 * CONV_WIDTH + QK_WIDTH, 3 * CONV_WIDTH + 2 * QK_WIDTH]
    gb, gc, xc, q, k, v = jnp.split(proj, splits, axis=-1)
    u = gc * xc
    u_ext = jnp.concatenate([conv_prefix.astype(u.dtype), u], axis=1)
    conv = u_ext[:, 0:s] * w_conv[0]
    for j in range(1, CONV_K):
        conv = conv + u_ext[:, j:j + s] * w_conv[j]
    y_conv = gb * conv
    new_conv = u_ext[:, -(CONV_K - 1):]
    pos = pos0 + jnp.arange(s, dtype=jnp.float32)
    q = rope_partial(q.reshape(b, s, N_HEADS, 2, HEAD_DIM), pos)
    k = rope_partial(k.reshape(b, s, N_HEADS, 2, HEAD_DIM), pos)
    v = v.reshape(b, s, N_HEADS, V_DIM)
    lf = lam_qk.astype(jnp.float32)
    lam = jnp.exp(jnp.sum(lf[0] * lf[1])) - jnp.exp(jnp.sum(lf[2] * lf[3])) + lambda_init
    if past_k is None:
        nb = s // Q_BLOCK
        qb = jnp.moveaxis(q.reshape(b, nb, Q_BLOCK, N_HEADS, 2, HEAD_DIM), 1, 0)
        k_pos = jnp.arange(s, dtype=jnp.int32)

        def one_block(args):
            qblk, i = args
            return diff_attention(qblk, k, v, i * Q_BLOCK + jnp.arange(Q_BLOCK, dtype=jnp.int32), k_pos, lam)

        o = lax.map(one_block, (qb, jnp.arange(nb, dtype=jnp.int32)))
        o = jnp.moveaxis(o, 0, 1).reshape(b, s, N_HEADS, V_DIM)
    else:
        n_past = past_k.shape[1]
        k_all = jnp.concatenate([past_k, k.astype(past_k.dtype)], axis=1)
        v_all = jnp.concatenate([past_v, v.astype(past_v.dtype)], axis=1)
        o = diff_attention(q, k_all, v_all, n_past + jnp.arange(s, dtype=jnp.int32),
                           jnp.arange(n_past + s, dtype=jnp.int32), lam)
    o = rms_norm(o, subln_g) * (1.0 - lambda_init)
    y = jnp.concatenate([y_conv, o.reshape(b, s, ATTN_WIDTH).astype(h.dtype)], axis=-1) @ w_o
    return y, k, v, new_conv


def trunk_layer(x, c, l, conv_prefix, past_k, past_v, pos0,
                w_ada, b_ada, ln_g, ln_b, ffn_w_up, ffn_w_down,
                w_in, w_conv, lambda_qk, subln_g, w_o):
    b = x.shape[0]
    ada = (jax.nn.silu(c) @ w_ada[l] + b_ada[l]).reshape(b, 3 * N_SUB, D_MODEL)[:, :, None, :]
    lambda_init = 0.8 - 0.6 * math.exp(-0.3 * l)

    def post(x, f, i):
        gate = 1.0 + ada[:, 3 * i + 2]
        return layer_norm(DEEPNORM_ALPHA * x + gate * f, ln_g[l, i], ln_b[l, i])

    def mod(x, i):
        return x * (1.0 + ada[:, 3 * i + 1]) + ada[:, 3 * i]

    x = post(x, 0.5 * swiglu(mod(x, 0), ffn_w_up[l, 0], ffn_w_down[l, 0]), 0)
    y, k, v, new_conv = token_mixer(mod(x, 1), conv_prefix, past_k, past_v, pos0,
                                    w_in[l], w_conv[l], lambda_qk[l], subln_g[l], w_o[l], lambda_init)
    x = post(x, y, 1)
    x = post(x, 0.5 * swiglu(mod(x, 2), ffn_w_up[l, 1], ffn_w_down[l, 1]), 2)
    return x, k, v, new_conv


def setup_inputs(seed: int = 0) -> dict:
    key = jax.random.key(seed)
    ks = jax.random.split(key, 20)
    f32 = jnp.float32
    n_pages = PAST_LEN // PAGE_SIZE
    n_used = DEC_BATCH * n_pages
    n_pool = n_used + max(1, n_used // 4)
    page_table = jax.random.permutation(ks[0], n_pool)[:n_used].reshape(DEC_BATCH, n_pages).astype(jnp.int32)
    x_prompt = jax.random.normal(ks[1], (BATCH, SEQ, D_MODEL), f32)
    x_sample = jax.random.normal(ks[2], (DEC_BATCH, DEC_SEQ, D_MODEL), f32)
    cache_k = jax.random.normal(ks[3], (DEPTH, n_pool, PAGE_SIZE, N_HEADS, 2, HEAD_DIM), f32)
    cache_v = jax.random.normal(ks[4], (DEPTH, n_pool, PAGE_SIZE, N_HEADS, V_DIM), f32) * DEEPNORM_BETA
    state_conv = jax.random.normal(ks[5], (DEPTH, DEC_BATCH, CONV_K - 1, CONV_WIDTH), f32) * 0.5
    c_prompt = jax.random.normal(ks[6], (BATCH, D_MODEL), f32)
    c_sample = jax.random.normal(ks[7], (DEC_BATCH, D_MODEL), f32)
    w_ada = jax.random.normal(ks[8], (DEPTH, D_MODEL, 3 * N_SUB * D_MODEL), f32) * (0.1 * D_MODEL ** -0.5)
    b_ada = jax.random.normal(ks[9], (DEPTH, 3 * N_SUB * D_MODEL), f32) * 0.01
    ln_g = 1.0 + 0.02 * jax.random.normal(ks[10], (DEPTH, N_SUB, D_MODEL), f32)
    ln_b = 0.02 * jax.random.normal(ks[11], (DEPTH, N_SUB, D_MODEL), f32)
    ffn_w_up = jax.random.normal(ks[12], (DEPTH, 2, D_MODEL, 2 * D_FF), f32) * D_MODEL ** -0.5
    ffn_w_down = jax.random.normal(ks[13], (DEPTH, 2, D_FF, D_MODEL), f32) * (D_FF ** -0.5 * DEEPNORM_BETA)
    col_scale = jnp.concatenate([jnp.ones((3 * CONV_WIDTH + 2 * QK_WIDTH,), f32),
                                 jnp.full((ATTN_WIDTH,), DEEPNORM_BETA, f32)])
    w_in = jax.random.normal(ks[14], (DEPTH, D_MODEL, N_IN), f32) * D_MODEL ** -0.5 * col_scale
    w_conv = jax.random.normal(ks[15], (DEPTH, CONV_K, CONV_WIDTH), f32) * CONV_K ** -0.5
    lambda_qk = jax.random.normal(ks[16], (DEPTH, 4, HEAD_DIM), f32) * 0.1
    subln_g = 1.0 + 0.02 * jax.random.normal(ks[17], (DEPTH, V_DIM), f32)
    w_o = jax.random.normal(ks[18], (DEPTH, D_MIX, D_MODEL), f32) * (D_MIX ** -0.5 * DEEPNORM_BETA)
    return {"x_prompt": x_prompt, "x_sample": x_sample, "cache_k": cache_k, "cache_v": cache_v,
            "state_conv": state_conv, "page_table": page_table, "c_prompt": c_prompt, "c_sample": c_sample,
            "w_ada": w_ada, "b_ada": b_ada, "ln_g": ln_g, "ln_b": ln_b,
            "ffn_w_up": ffn_w_up, "ffn_w_down": ffn_w_down, "w_in": w_in, "w_conv": w_conv,
            "lambda_qk": lambda_qk, "subln_g": subln_g, "w_o": w_o}


def reference(x_prompt, x_sample, cache_k, cache_v, state_conv, page_table, c_prompt, c_sample,
              w_ada, b_ada, ln_g, ln_b, ffn_w_up, ffn_w_down, w_in, w_conv, lambda_qk, subln_g, w_o):
    weights = (w_ada, b_ada, ln_g, ln_b, ffn_w_up, ffn_w_down, w_in, w_conv, lambda_qk, subln_g, w_o)
    n_seq, n_pages = page_table.shape
    past_len = n_pages * cache_k.shape[2]
    xp = x_prompt
    xs = x_sample
    kp_l, vp_l, cp_l, ks_l, vs_l, cs_l = [], [], [], [], [], []
    for l in range(DEPTH):
        prefix0 = jnp.zeros((xp.shape[0], CONV_K - 1, CONV_WIDTH), xp.dtype)
        xp, kp, vp, cp = trunk_layer(xp, c_prompt, l, prefix0, None, None, 0, *weights)
        past_k = cache_k[l, page_table].reshape(n_seq, past_len, N_HEADS, 2, HEAD_DIM)
        past_v = cache_v[l, page_table].reshape(n_seq, past_len, N_HEADS, V_DIM)
        xs, ksn, vsn, csn = trunk_layer(xs, c_sample, l, state_conv[l], past_k, past_v, past_len, *weights)
        kp_l.append(kp); vp_l.append(vp); cp_l.append(cp)
        ks_l.append(ksn); vs_l.append(vsn); cs_l.append(csn)
    k_prompt = jnp.stack(kp_l)
    v_prompt = jnp.stack(vp_l)
    conv_prompt = jnp.stack(cp_l)
    k_sample = jnp.stack(ks_l)
    v_sample = jnp.stack(vs_l)
    conv_sample = jnp.stack(cs_l)
    return (xp, xs, k_prompt, v_prompt, conv_prompt, k_sample, v_sample, conv_sample)
```

```python
import functools
import math

import jax
import jax.numpy as jnp
from jax import lax
from jax.experimental import pallas as pl
from jax.experimental.pallas import tpu as pltpu

N_HEADS = 4
HEAD_DIM = 64
V_DIM = 2 * HEAD_DIM
CONV_K = 3
ROT_DIM = HEAD_DIM // 4
ROPE_THETA = 500000.0
LN_EPS = 1e-5
RMS_EPS = 1e-5
N_SUB = 3

LANES = 128
SUBLANES = 8
V7X_VMEM_BYTES = 64 * 1024 * 1024
VMEM_LIMIT_BYTES = V7X_VMEM_BYTES - 8 * 1024 * 1024

F32 = jnp.float32
BF16 = jnp.bfloat16


def _compiler_params(n_grid_dims):
    return pltpu.CompilerParams(
        dimension_semantics=("arbitrary",) * n_grid_dims,
        vmem_limit_bytes=VMEM_LIMIT_BYTES,
    )


def _const_spec(shape, index_map):
    return pl.BlockSpec(shape, index_map, pipeline_mode=pl.Buffered(1))


def _layer_norm(y, g, b):
    mu = jnp.mean(y, axis=-1, keepdims=True)
    d = y - mu
    var = jnp.mean(d * d, axis=-1, keepdims=True)
    return d * lax.rsqrt(var + LN_EPS) * g + b


def _ada_vectors(ada_ref, sub, per_row):
    if per_row:
        return ada_ref[3 * sub], ada_ref[3 * sub + 1], ada_ref[3 * sub + 2]
    a = ada_ref[0]
    return a[3 * sub:3 * sub + 1], a[3 * sub + 1:3 * sub + 2], a[3 * sub + 2:3 * sub + 3]


def _ada_spec(per_row, tm, d_model):
    if per_row:
        return pl.BlockSpec((3 * N_SUB, tm, d_model), lambda g, j: (0, j, 0))
    return pl.BlockSpec((1, 3 * N_SUB, d_model), lambda g, j: (g, 0, 0))


def _adaln_kernel(c_ref, w_ref, b_ref, o_ref):
    c = c_ref[...]
    h = (c * jax.nn.sigmoid(c)).astype(BF16)
    o_ref[0] = jnp.dot(h, w_ref[0].astype(BF16), preferred_element_type=F32) + b_ref[0]


def _adaln(c_all, w_ada, b_ada):
    depth, d_model, n_out = w_ada.shape
    rows = c_all.shape[0]
    tn = 1024 if n_out % 1024 == 0 else n_out
    return pl.pallas_call(
        _adaln_kernel,
        grid=(depth, n_out // tn),
        in_specs=[
            pl.BlockSpec((rows, d_model), lambda l, n: (0, 0)),
            pl.BlockSpec((1, d_model, tn), lambda l, n: (l, 0, n)),
            pl.BlockSpec((1, 1, tn), lambda l, n: (l, 0, n)),
        ],
        out_specs=pl.BlockSpec((1, rows, tn), lambda l, n: (l, 0, n)),
        out_shape=jax.ShapeDtypeStruct((depth, rows, n_out), F32),
        compiler_params=_compiler_params(2),
        name="adaln",
    )(c_all, w_ada, b_ada.reshape(depth, 1, n_out))


def _rope_table_kernel(pos_ref, inv_ref, cos_ref, sa_ref, sb_ref):
    ang = pos_ref[...] * inv_ref[...]
    lane = lax.broadcasted_iota(jnp.int32, ang.shape, 1) % HEAD_DIM
    half = ROT_DIM // 2
    c = jnp.cos(ang)
    s = jnp.sin(ang)
    cos_ref[...] = jnp.where(lane < ROT_DIM, c, 1.0)
    sa_ref[...] = jnp.where(lane < half, -s, 0.0)
    sb_ref[...] = jnp.where((lane >= half) & (lane < ROT_DIM), s, 0.0)


def _rope_tables(pos):
    rows = pos.shape[0]
    half = ROT_DIM // 2
    lane = jnp.arange(LANES, dtype=jnp.int32) % HEAD_DIM % half
    inv_freq = jnp.power(ROPE_THETA, -(2 * lane).astype(F32) / ROT_DIM).reshape(1, LANES)
    shp = jax.ShapeDtypeStruct((rows, LANES), F32)
    return pl.pallas_call(
        _rope_table_kernel,
        out_shape=(shp, shp, shp),
        name="rope_tables",
    )(pos.reshape(rows, 1), inv_freq)


def _ffn_kernel(*refs, sub, per_row, alpha, d_ff, ck, with_outproj):
    if with_outproj:
        (x_ref, yc_ref, o_ref_in, ada_ref, wo_ref, wup_ref, wdn_ref, lng_ref, lnb_ref,
         out_ref, hid_scr) = refs
    else:
        x_ref, ada_ref, wup_ref, wdn_ref, lng_ref, lnb_ref, out_ref, hid_scr = refs
    x = x_ref[0]
    if with_outproj:
        _, _, gate1 = _ada_vectors(ada_ref, 1, per_row)
        mix = jnp.concatenate([yc_ref[0], o_ref_in[0]], axis=-1)
        y = jnp.dot(mix, wo_ref[...], preferred_element_type=F32)
        x = _layer_norm(alpha * x + (1.0 + gate1) * y, lng_ref[0:1], lnb_ref[0:1])
        ln_row = 1
    else:
        ln_row = 0
    shift, scale, gate = _ada_vectors(ada_ref, sub, per_row)
    h = (x * (1.0 + scale) + shift).astype(BF16)
    for j in range(d_ff // ck):
        g = jnp.dot(h, wup_ref[:, j * ck:(j + 1) * ck], preferred_element_type=F32)
        u = jnp.dot(h, wup_ref[:, d_ff + j * ck:d_ff + (j + 1) * ck], preferred_element_type=F32)
        hid_scr[:, j * ck:(j + 1) * ck] = (g * jax.nn.sigmoid(g) * u).astype(BF16)
    f = jnp.dot(hid_scr[...], wdn_ref[...], preferred_element_type=F32)
    y = alpha * x + (1.0 + gate) * (0.5 * f)
    out_ref[0] = _layer_norm(y, lng_ref[ln_row:ln_row + 1], lnb_ref[ln_row:ln_row + 1])


def _ffn(x, ada, w_up, w_dn, ln_g, ln_b, *, layer, which, sub, per_row, alpha, tm, mix=None, w_o=None):
    n_groups, rows, d_model = x.shape
    d_ff = w_dn.shape[2]
    ck = 256 if d_ff % 256 == 0 else d_ff
    with_outproj = mix is not None
    row_spec = pl.BlockSpec((1, tm, d_model), lambda g, j: (g, j, 0))
    in_specs = [row_spec]
    args = [x]
    if with_outproj:
        width = mix[0].shape[-1]
        mix_spec = pl.BlockSpec((1, tm, width), lambda g, j: (g, j, 0))
        in_specs += [mix_spec, mix_spec]
        args += list(mix)
    in_specs.append(_ada_spec(per_row, tm, d_model))
    args.append(ada)
    if with_outproj:
        in_specs.append(_const_spec((None, w_o.shape[1], d_model), lambda g, j: (layer, 0, 0)))
        args.append(w_o)
    in_specs += [
        _const_spec((None, None, d_model, 2 * d_ff), lambda g, j: (layer, which, 0, 0)),
        _const_spec((None, None, d_ff, d_model), lambda g, j: (layer, which, 0, 0)),
        _const_spec(ln_g.shape, lambda g, j: (0, 0)),
        _const_spec(ln_b.shape, lambda g, j: (0, 0)),
    ]
    args += [w_up, w_dn, ln_g, ln_b]
    return pl.pallas_call(
        functools.partial(_ffn_kernel, sub=sub, per_row=per_row, alpha=alpha, d_ff=d_ff, ck=ck,
                          with_outproj=with_outproj),
        grid=(n_groups, rows // tm),
        in_specs=in_specs,
        out_specs=row_spec,
        out_shape=jax.ShapeDtypeStruct(x.shape, F32),
        scratch_shapes=[pltpu.VMEM((tm, d_ff), BF16)],
        compiler_params=_compiler_params(2),
        name="ffn_outproj" if with_outproj else "ffn",
    )(*args)


def _rope(z, cos, sa, sb):
    half = ROT_DIM // 2
    outs = []
    for s in range(z.shape[-1] // LANES):
        zs = z[:, s * LANES:(s + 1) * LANES]
        up = pltpu.roll(zs, LANES - half, 1)
        dn = pltpu.roll(zs, half, 1)
        outs.append(zs * cos + up * sa + dn * sb)
    return jnp.concatenate(outs, axis=-1)


def _mixer_in_kernel(*refs, per_row, rows_per_seq, tm, cw, qk_w):
    if per_row:
        (x_ref, ada_ref, win_ref, wconv_ref, cos_ref, sa_ref, sb_ref, p1_ref, p2_ref,
         yc_ref, qz_ref, kf_ref, vf_ref, kb_ref, vb_ref, conv_ref, uext_scr) = refs
    else:
        (x_ref, ada_ref, win_ref, wconv_ref, cos_ref, sa_ref, sb_ref,
         yc_ref, qz_ref, kf_ref, vf_ref, kb_ref, vb_ref, conv_ref, uext_scr) = refs
    j = pl.program_id(1)
    x = x_ref[0]
    shift, scale, _ = _ada_vectors(ada_ref, 1, per_row)
    h = (x * (1.0 + scale) + shift).astype(BF16)

    def proj(lo, width):
        return jnp.dot(h, win_ref[:, lo:lo + width], preferred_element_type=F32)

    gb = proj(0, cw)
    gc = proj(cw, cw)
    xc = proj(2 * cw, cw)
    u = gc * xc

    @pl.when(j == 0)
    def _():
        uext_scr[0:SUBLANES, :] = jnp.zeros((SUBLANES, cw), F32)

    @pl.when(j > 0)
    def _():
        uext_scr[0:SUBLANES, :] = uext_scr[tm:tm + SUBLANES, :]

    uext_scr[SUBLANES:SUBLANES + tm, :] = u
    um2 = uext_scr[SUBLANES - 2:SUBLANES - 2 + tm, :]
    um1 = uext_scr[SUBLANES - 1:SUBLANES - 1 + tm, :]
    if per_row:
        t = lax.broadcasted_iota(jnp.int32, (tm, cw), 0) % rows_per_seq
        um2 = jnp.where(t < 2, p2_ref[0], um2)
        um1 = jnp.where(t < 1, p1_ref[0], um1)
    wc = wconv_ref[...]
    conv = um2 * wc[0:1] + um1 * wc[1:2] + u * wc[2:3]
    yc_ref[0] = (gb * conv).astype(BF16)
    if per_row:
        conv_ref[0] = u
    else:
        conv_ref[0] = uext_scr[SUBLANES + tm - (CONV_K - 1):SUBLANES + tm, :]

    cos, sa, sb = cos_ref[...], sa_ref[...], sb_ref[...]
    q = _rope(proj(3 * cw, qk_w), cos, sa, sb) * (HEAD_DIM ** -0.5)
    first = (lax.broadcasted_iota(jnp.int32, q.shape, 1) % V_DIM) < HEAD_DIM
    qz_ref[0, 0] = jnp.where(first, q, 0.0).astype(BF16)
    qz_ref[0, 1] = jnp.where(first, 0.0, q).astype(BF16)
    k = _rope(proj(3 * cw + qk_w, qk_w), cos, sa, sb)
    kf_ref[0] = k
    kb_ref[0] = k.astype(BF16)
    v = proj(3 * cw + 2 * qk_w, qk_w)
    vf_ref[0] = v
    vb_ref[0] = v.astype(BF16)


def _mixer_in(x, ada, w_in, w_conv, tables, *, layer, per_row, rows_per_seq, tm, prefix=None):
    n_groups, rows, d_model = x.shape
    cw = w_conv.shape[-1]
    n_in = w_in.shape[-1]
    qk_w = (n_in - 3 * cw) // 3
    assert qk_w == N_HEADS * V_DIM and n_in == 3 * cw + 3 * qk_w
    row_spec = pl.BlockSpec((1, tm, d_model), lambda g, j: (g, j, 0))
    tab_spec = pl.BlockSpec((tm, LANES), lambda g, j: (j, 0))
    in_specs = [
        row_spec,
        _ada_spec(per_row, tm, d_model),
        _const_spec((None, d_model, n_in), lambda g, j: (layer, 0, 0)),
        _const_spec((None, CONV_K, cw), lambda g, j: (layer, 0, 0)),
        tab_spec, tab_spec, tab_spec,
    ]
    args = [x, ada, w_in, w_conv, *tables]
    w_spec = pl.BlockSpec((1, tm, cw), lambda g, j: (g, j, 0))
    if per_row:
        in_specs += [w_spec, w_spec]
        args += list(prefix)
        conv_spec, conv_shape = w_spec, (n_groups, rows, cw)
    else:
        conv_spec = pl.BlockSpec((1, CONV_K - 1, cw), lambda g, j: (g, 0, 0))
        conv_shape = (n_groups, CONV_K - 1, cw)
    qk_spec = pl.BlockSpec((1, tm, qk_w), lambda g, j: (g, j, 0))
    out_specs = [
        w_spec,
        pl.BlockSpec((1, 2, tm, qk_w), lambda g, j: (g, 0, j, 0)),
        qk_spec, qk_spec, qk_spec, qk_spec,
        conv_spec,
    ]
    out_shape = [
        jax.ShapeDtypeStruct((n_groups, rows, cw), BF16),
        jax.ShapeDtypeStruct((n_groups, 2, rows, qk_w), BF16),
        jax.ShapeDtypeStruct((n_groups, rows, qk_w), F32),
        jax.ShapeDtypeStruct((n_groups, rows, qk_w), F32),
        jax.ShapeDtypeStruct((n_groups, rows, qk_w), BF16),
        jax.ShapeDtypeStruct((n_groups, rows, qk_w), BF16),
        jax.ShapeDtypeStruct(conv_shape, F32),
    ]
    return pl.pallas_call(
        functools.partial(_mixer_in_kernel, per_row=per_row, rows_per_seq=rows_per_seq, tm=tm, cw=cw, qk_w=qk_w),
        grid=(n_groups, rows // tm),
        in_specs=in_specs,
        out_specs=out_specs,
        out_shape=out_shape,
        scratch_shapes=[pltpu.VMEM((tm + 2 * SUBLANES, cw), F32)],
        compiler_params=_compiler_params(2),
        name="mixer_in",
    )(*args)


def _lambda_value(lam_ref, lambda_init):
    lf = lam_ref[...]
    a = jnp.sum(lf[0:1] * lf[1:2], axis=-1, keepdims=True)
    b = jnp.sum(lf[2:3] * lf[3:4], axis=-1, keepdims=True)
    return jnp.exp(a) - jnp.exp(b) + lambda_init


def _online_softmax_step(s, v, m_prev, l_prev, acc_prev):
    n_rep = s.shape[-1] // LANES
    m_next = jnp.maximum(m_prev, jnp.max(s, axis=-1, keepdims=True))
    p = jnp.exp(s - jnp.concatenate([m_next] * n_rep, axis=-1))
    alpha = jnp.exp(m_prev - m_next)
    l_next = alpha * l_prev + jnp.sum(p, axis=-1, keepdims=True)
    acc_next = alpha * acc_prev + jnp.dot(p.astype(BF16), v, preferred_element_type=F32)
    return m_next, l_next, acc_next


def _sub_layer_norm(o, g, lambda_init):
    o = o * lax.rsqrt(jnp.mean(o * o, axis=-1, keepdims=True) + RMS_EPS) * g
    return o * (1.0 - lambda_init)


_CONTRACT_LAST = (((1,), (1,)), ((), ()))


def _attn_kernel(q_ref, k_ref, v_ref, lam_ref, g_ref, o_ref, m_scr, l_scr, acc_scr, *, tq, lambda_init):
    qi = pl.program_id(2)
    q = q_ref[0].reshape(2 * tq, V_DIM)
    m_scr[...] = jnp.full(m_scr.shape, -jnp.inf, F32)
    l_scr[...] = jnp.zeros(l_scr.shape, F32)
    acc_scr[...] = jnp.zeros(acc_scr.shape, F32)

    def step(j, masked):
        start = pl.multiple_of(j * tq, tq)
        k = k_ref[0, pl.ds(start, tq), :]
        v = v_ref[0, pl.ds(start, tq), :]
        s = lax.dot_general(q, k, _CONTRACT_LAST, preferred_element_type=F32)
        if masked:
            row = lax.broadcasted_iota(jnp.int32, s.shape, 0) % tq
            col = lax.broadcasted_iota(jnp.int32, s.shape, 1)
            s = jnp.where(col <= row, s, -jnp.inf)
        m, l, acc = _online_softmax_step(s, v, m_scr[...], l_scr[...], acc_scr[...])
        m_scr[...] = m
        l_scr[...] = l
        acc_scr[...] = acc

    def body(j, carry):
        step(j, False)
        return carry

    lax.fori_loop(0, qi, body, 0)
    step(qi, True)

    a = acc_scr[...] / l_scr[...]
    o = a[:tq] - _lambda_value(lam_ref, lambda_init) * a[tq:]
    o_ref[0] = _sub_layer_norm(o, g_ref[...], lambda_init).astype(o_ref.dtype)


def _prompt_attention(qz, k, v, lam_qk, subln_g, *, lambda_init, tq):
    batch, _, seq, width = qz.shape
    kv_spec = pl.BlockSpec((1, seq, V_DIM), lambda b, h, i: (b, 0, h))
    return pl.pallas_call(
        functools.partial(_attn_kernel, tq=tq, lambda_init=lambda_init),
        grid=(batch, width // V_DIM, seq // tq),
        in_specs=[
            pl.BlockSpec((1, 2, tq, V_DIM), lambda b, h, i: (b, 0, i, h)),
            kv_spec, kv_spec,
            pl.BlockSpec(lam_qk.shape, lambda b, h, i: (0, 0)),
            pl.BlockSpec(subln_g.shape, lambda b, h, i: (0, 0)),
        ],
        out_specs=pl.BlockSpec((1, tq, V_DIM), lambda b, h, i: (b, i, h)),
        out_shape=jax.ShapeDtypeStruct((batch, seq, width), BF16),
        scratch_shapes=[pltpu.VMEM((2 * tq, LANES), F32)] * 3,
        compiler_params=_compiler_params(3),
        name="prompt_attention",
    )(qz, k, v, lam_qk, subln_g)


def _decode_kernel(pt_ref, q_ref, *refs, pages_per_step, page, n_new, lambda_init):
    del pt_ref
    npg = pages_per_step
    k_refs, v_refs = refs[:npg], refs[npg:2 * npg]
    knew_ref, vnew_ref, lam_ref, g_ref, o_ref, kbf_scr, vbf_scr, m_scr, l_scr, acc_scr = refs[2 * npg:]
    p = pl.program_id(1)
    rows = q_ref.shape[2]

    @pl.when(p == 0)
    def _():
        m_scr[...] = jnp.full(m_scr.shape, -jnp.inf, F32)
        l_scr[...] = jnp.zeros(l_scr.shape, F32)
        acc_scr[...] = jnp.zeros(acc_scr.shape, F32)

    def update(h, k_t, v, mask=None):
        q = q_ref[0, h].astype(BF16)
        s = jnp.dot(q, k_t, preferred_element_type=F32)
        if mask is not None:
            s = jnp.where(mask, s, -jnp.inf)
        m, l, acc = _online_softmax_step(s, v, m_scr[h], l_scr[h], acc_scr[h])
        m_scr[h] = m
        l_scr[h] = l
        acc_scr[h] = acc

    for i in range(npg):
        kbf_scr[:, i * page:(i + 1) * page] = k_refs[i][0, 0].astype(BF16)
        for h in range(N_HEADS):
            vbf_scr[h, i * page:(i + 1) * page, :] = (
                v_refs[i][0, 0, pl.ds(h, page, stride=N_HEADS), :].astype(BF16))
    for h in range(N_HEADS):
        update(h, kbf_scr[h * V_DIM:(h + 1) * V_DIM, :], vbf_scr[h])

    @pl.when(p == pl.num_programs(1) - 1)
    def _():
        row_t = lax.broadcasted_iota(jnp.int32, (rows, LANES), 0) % n_new
        col = lax.broadcasted_iota(jnp.int32, (rows, LANES), 1)
        mask = col <= row_t
        lam = _lambda_value(lam_ref, lambda_init)
        for h in range(N_HEADS):
            update(h, knew_ref[0, h * V_DIM:(h + 1) * V_DIM, :].astype(BF16),
                   vnew_ref[0, :, h * V_DIM:(h + 1) * V_DIM].astype(BF16), mask)
            a = acc_scr[h] / l_scr[h]
            o = a - lam * pltpu.roll(a, rows - n_new, 0)
            o_ref[0, h] = _sub_layer_norm(o, g_ref[...], lambda_init)


def _decode_attention(qd, cache_k, cache_v, page_table, k_new, v_new, lam_qk, subln_g, *, layer, lambda_init,
                      pages_per_step):
    n_seq, n_heads, rows, _ = qd.shape
    n_new = rows // 2
    _, _, width, page = cache_k.shape
    n_pages = page_table.shape[1]
    npg = pages_per_step
    assert n_pages % npg == 0 and n_heads == N_HEADS and rows % SUBLANES == 0 and page == LANES

    def k_spec(i):
        return pl.BlockSpec((1, 1, width, page), lambda n, p, pt: (layer, pt[n, p * npg + i], 0, 0))

    def v_spec(i):
        return pl.BlockSpec((1, 1, page * n_heads, V_DIM), lambda n, p, pt: (layer, pt[n, p * npg + i], 0, 0))

    q_spec = pl.BlockSpec((1, n_heads, rows, V_DIM), lambda n, p, pt: (n, 0, 0, 0))
    grid_spec = pltpu.PrefetchScalarGridSpec(
        num_scalar_prefetch=1,
        grid=(n_seq, n_pages // npg),
        in_specs=[q_spec] + [k_spec(i) for i in range(npg)] + [v_spec(i) for i in range(npg)] + [
            pl.BlockSpec((1, width, LANES), lambda n, p, pt: (n, 0, 0)),
            pl.BlockSpec((1, LANES, width), lambda n, p, pt: (n, 0, 0)),
            pl.BlockSpec(lam_qk.shape, lambda n, p, pt: (0, 0)),
            pl.BlockSpec(subln_g.shape, lambda n, p, pt: (0, 0)),
        ],
        out_specs=q_spec,
        scratch_shapes=[pltpu.VMEM((width, npg * page), BF16), pltpu.VMEM((n_heads, npg * page, V_DIM), BF16)]
        + [pltpu.VMEM((n_heads, rows, LANES), F32)] * 3,
    )
    return pl.pallas_call(
        functools.partial(_decode_kernel, pages_per_step=npg, page=page, n_new=n_new, lambda_init=lambda_init),
        grid_spec=grid_spec,
        out_shape=jax.ShapeDtypeStruct(qd.shape, F32),
        compiler_params=_compiler_params(2),
        name="decode_attention",
    )(page_table, qd, *([cache_k] * npg), *([cache_v] * npg), k_new, v_new, lam_qk, subln_g)


def _pick_tile(rows, target):
    tile = min(rows, target)
    while rows % tile:
        tile //= 2
    return tile


def kernel(x_prompt, x_sample, cache_k, cache_v, state_conv, page_table, c_prompt, c_sample, w_ada, b_ada, ln_g, ln_b, ffn_w_up, ffn_w_down, w_in, w_conv, lambda_qk, subln_g, w_o):
    depth = w_ada.shape[0]
    batch, seq, d_model = x_prompt.shape
    n_seq, n_new, _ = x_sample.shape
    _, n_pool, page, n_heads, _, head_dim = cache_k.shape
    assert n_heads == N_HEADS and head_dim == HEAD_DIM and page == LANES
    cw = w_conv.shape[-1]
    qk_w = N_HEADS * V_DIM
    past_len = page_table.shape[1] * page
    alpha = (2.0 * depth) ** 0.25
    s_rows = n_seq * n_new

    w_up_b = ffn_w_up.astype(BF16)
    w_dn_b = ffn_w_down.astype(BF16)
    w_in_b = w_in.astype(BF16)
    w_o_b = w_o.astype(BF16)
    cache_k2 = jnp.transpose(cache_k, (0, 1, 3, 4, 5, 2)).reshape(depth, n_pool, qk_w, page)
    cache_v2 = cache_v.reshape(depth, n_pool, page * N_HEADS, V_DIM)

    ada_all = _adaln(jnp.concatenate([c_prompt, c_sample], axis=0), w_ada, b_ada)
    ada_all = ada_all.reshape(depth, batch + n_seq, 3 * N_SUB, d_model)

    tab_p = _rope_tables(jnp.arange(seq, dtype=F32))
    tab_s = _rope_tables(past_len + (jnp.arange(s_rows, dtype=jnp.int32) % n_new).astype(F32))

    tm_p = _pick_tile(seq, 512)
    tq = _pick_tile(seq, 256)
    pages_per_step = _pick_tile(page_table.shape[1], 8)

    xp = x_prompt
    xs = x_sample.reshape(1, s_rows, d_model)
    outs = {k: [] for k in ("kp", "vp", "cp", "ks", "vs", "cs")}
    for l in range(depth):
        lambda_init = 0.8 - 0.6 * math.exp(-0.3 * l)
        lam_l = lambda_qk[l]
        g_l = subln_g[l].reshape(1, V_DIM)
        ln_g0, ln_b0 = ln_g[l, 0:1], ln_b[l, 0:1]
        ln_g12, ln_b12 = ln_g[l, 1:3], ln_b[l, 1:3]

        ada_p = ada_all[l, :batch]
        xp = _ffn(xp, ada_p, w_up_b, w_dn_b, ln_g0, ln_b0, layer=l, which=0, sub=0, per_row=False,
                  alpha=alpha, tm=tm_p)
        yc, qz, kf, vf, kb, vb, conv = _mixer_in(xp, ada_p, w_in_b, w_conv, tab_p, layer=l, per_row=False,
                                                 rows_per_seq=seq, tm=tm_p)
        o = _prompt_attention(qz, kb, vb, lam_l, g_l, lambda_init=lambda_init, tq=tq)
        xp = _ffn(xp, ada_p, w_up_b, w_dn_b, ln_g12, ln_b12, layer=l, which=1, sub=2, per_row=False,
                  alpha=alpha, tm=tm_p, mix=(yc, o), w_o=w_o_b)
        outs["kp"].append(kf.reshape(batch, seq, N_HEADS, 2, HEAD_DIM))
        outs["vp"].append(vf.reshape(batch, seq, N_HEADS, V_DIM))
        outs["cp"].append(conv)

        ada_s = jnp.repeat(jnp.transpose(ada_all[l, batch:], (1, 0, 2)), n_new, axis=1)
        pre = state_conv[l]
        zeros = jnp.zeros((n_seq, n_new - 1, cw), F32)
        p1 = jnp.concatenate([pre[:, 1:2], zeros], axis=1).reshape(1, s_rows, cw)
        p2 = jnp.concatenate([pre, zeros[:, 1:]], axis=1).reshape(1, s_rows, cw)
        xs = _ffn(xs, ada_s, w_up_b, w_dn_b, ln_g0, ln_b0, layer=l, which=0, sub=0, per_row=True,
                  alpha=alpha, tm=s_rows)
        yc, qz, kf, vf, _, _, u_s = _mixer_in(xs, ada_s, w_in_b, w_conv, tab_s, layer=l, per_row=True,
                                              rows_per_seq=n_new, tm=s_rows, prefix=(p1, p2))
        qd = qz.astype(F32).reshape(2, n_seq, n_new, N_HEADS, V_DIM)
        qd = jnp.transpose(qd, (1, 3, 0, 2, 4)).reshape(n_seq, N_HEADS, 2 * n_new, V_DIM)
        pad = ((0, 0), (0, LANES - n_new), (0, 0))
        k_new = jnp.transpose(jnp.pad(kf.reshape(n_seq, n_new, qk_w), pad), (0, 2, 1))
        v_new = jnp.pad(vf.reshape(n_seq, n_new, qk_w), pad)
        od = _decode_attention(qd, cache_k2, cache_v2, page_table, k_new, v_new, lam_l, g_l, layer=l,
                               lambda_init=lambda_init, pages_per_step=pages_per_step)
        o = jnp.transpose(od[:, :, :n_new], (0, 2, 1, 3)).reshape(1, s_rows, qk_w).astype(BF16)
        xs = _ffn(xs, ada_s, w_up_b, w_dn_b, ln_g12, ln_b12, layer=l, which=1, sub=2, per_row=True,
                  alpha=alpha, tm=s_rows, mix=(yc, o), w_o=w_o_b)
        outs["ks"].append(kf.reshape(n_seq, n_new, N_HEADS, 2, HEAD_DIM))
        outs["vs"].append(vf.reshape(n_seq, n_new, N_HEADS, V_DIM))
        outs["cs"].append(u_s.reshape(n_seq, n_new, cw)[:, n_new - (CONV_K - 1):])

    return (xp, xs.reshape(n_seq, n_new, d_model),
            jnp.stack(outs["kp"]), jnp.stack(outs["vp"]), jnp.stack(outs["cp"]),
            jnp.stack(outs["ks"]), jnp.stack(outs["vs"]), jnp.stack(outs["cs"]))
```

```python
import functools
import math

import jax
import jax.numpy as jnp
from jax import lax
from jax.experimental import pallas as pl
from jax.experimental.pallas import tpu as pltpu

N_HEADS = 4
HEAD_DIM = 64
V_DIM = 2 * HEAD_DIM
CONV_K = 3
ROT_DIM = HEAD_DIM // 4
ROPE_THETA = 500000.0
LN_EPS = 1e-5
RMS_EPS = 1e-5
N_SUB = 3

LANES = 128
SUBLANES = 8
V7X_VMEM_BYTES = 64 * 1024 * 1024
VMEM_LIMIT_BYTES = V7X_VMEM_BYTES - 8 * 1024 * 1024

F32 = jnp.float32
BF16 = jnp.bfloat16


def _compiler_params(n_grid_dims):
    return pltpu.CompilerParams(
        dimension_semantics=("arbitrary",) * n_grid_dims,
        vmem_limit_bytes=VMEM_LIMIT_BYTES,
    )


def _const_spec(shape, index_map):
    return pl.BlockSpec(shape, index_map, pipeline_mode=pl.Buffered(1))


def _layer_norm(y, g, b):
    mu = jnp.mean(y, axis=-1, keepdims=True)
    d = y - mu
    var = jnp.mean(d * d, axis=-1, keepdims=True)
    return d * lax.rsqrt(var + LN_EPS) * g + b


def _ada_vectors(ada_ref, sub, per_row):
    if per_row:
        return ada_ref[3 * sub], ada_ref[3 * sub + 1], ada_ref[3 * sub + 2]
    a = ada_ref[0]
    return a[3 * sub:3 * sub + 1], a[3 * sub + 1:3 * sub + 2], a[3 * sub + 2:3 * sub + 3]


def _ada_spec(per_row, tm, d_model):
    if per_row:
        return pl.BlockSpec((3 * N_SUB, tm, d_model), lambda g, j: (0, j, 0))
    return pl.BlockSpec((1, 3 * N_SUB, d_model), lambda g, j: (g, 0, 0))


def _adaln_kernel(c_ref, w_ref, b_ref, o_ref):
    c = c_ref[...]
    h = (c * jax.nn.sigmoid(c)).astype(BF16)
    o_ref[0] = jnp.dot(h, w_ref[0].astype(BF16), preferred_element_type=F32) + b_ref[0]


def _adaln(c_all, w_ada, b_ada):
    depth, d_model, n_out = w_ada.shape
    rows = c_all.shape[0]
    tn = 1024 if n_out % 1024 == 0 else n_out
    return pl.pallas_call(
        _adaln_kernel,
        grid=(depth, n_out // tn),
        in_specs=[
            pl.BlockSpec((rows, d_model), lambda l, n: (0, 0)),
            pl.BlockSpec((1, d_model, tn), lambda l, n: (l, 0, n)),
            pl.BlockSpec((1, 1, tn), lambda l, n: (l, 0, n)),
        ],
        out_specs=pl.BlockSpec((1, rows, tn), lambda l, n: (l, 0, n)),
        out_shape=jax.ShapeDtypeStruct((depth, rows, n_out), F32),
        compiler_params=_compiler_params(2),
        name="adaln",
    )(c_all, w_ada, b_ada.reshape(depth, 1, n_out))


def _rope_table_kernel(pos_ref, inv_ref, cos_ref, sa_ref, sb_ref):
    ang = pos_ref[...] * inv_ref[...]
    lane = lax.broadcasted_iota(jnp.int32, ang.shape, 1) % HEAD_DIM
    half = ROT_DIM // 2
    c = jnp.cos(ang)
    s = jnp.sin(ang)
    cos_ref[...] = jnp.where(lane < ROT_DIM, c, 1.0)
    sa_ref[...] = jnp.where(lane < half, -s, 0.0)
    sb_ref[...] = jnp.where((lane >= half) & (lane < ROT_DIM), s, 0.0)


def _rope_tables(pos):
    rows = pos.shape[0]
    half = ROT_DIM // 2
    lane = jnp.arange(LANES, dtype=jnp.int32) % HEAD_DIM % half
    inv_freq = jnp.power(ROPE_THETA, -(2 * lane).astype(F32) / ROT_DIM).reshape(1, LANES)
    shp = jax.ShapeDtypeStruct((rows, LANES), F32)
    return pl.pallas_call(
        _rope_table_kernel,
        out_shape=(shp, shp, shp),
        name="rope_tables",
    )(pos.reshape(rows, 1), inv_freq)


def _ffn_kernel(*refs, sub, per_row, alpha, d_ff, ck, with_outproj):
    if with_outproj:
        (x_ref, yc_ref, o_ref_in, ada_ref, wo_ref, wup_ref, wdn_ref, lng_ref, lnb_ref,
         out_ref, hid_scr) = refs
    else:
        x_ref, ada_ref, wup_ref, wdn_ref, lng_ref, lnb_ref, out_ref, hid_scr = refs
    x = x_ref[0]
    if with_outproj:
        _, _, gate1 = _ada_vectors(ada_ref, 1, per_row)
        mix = jnp.concatenate([yc_ref[0], o_ref_in[0]], axis=-1)
        y = jnp.dot(mix, wo_ref[...], preferred_element_type=F32)
        x = _layer_norm(alpha * x + (1.0 + gate1) * y, lng_ref[0:1], lnb_ref[0:1])
        ln_row = 1
    else:
        ln_row = 0
    shift, scale, gate = _ada_vectors(ada_ref, sub, per_row)
    h = (x * (1.0 + scale) + shift).astype(BF16)
    for j in range(d_ff // ck):
        g = jnp.dot(h, wup_ref[:, j * ck:(j + 1) * ck], preferred_element_type=F32)
        u = jnp.dot(h, wup_ref[:, d_ff + j * ck:d_ff + (j + 1) * ck], preferred_element_type=F32)
        hid_scr[:, j * ck:(j + 1) * ck] = (g * jax.nn.sigmoid(g) * u).astype(BF16)
    f = jnp.dot(hid_scr[...], wdn_ref[...], preferred_element_type=F32)
    y = alpha * x + (1.0 + gate) * (0.5 * f)
    out_ref[0] = _layer_norm(y, lng_ref[ln_row:ln_row + 1], lnb_ref[ln_row:ln_row + 1])


def _ffn(x, ada, w_up, w_dn, ln_g, ln_b, *, layer, which, sub, per_row, alpha, tm, mix=None, w_o=None):
    n_groups, rows, d_model = x.shape
    d_ff = w_dn.shape[2]
    ck = 256 if d_ff % 256 == 0 else d_ff
    with_outproj = mix is not None
    row_spec = pl.BlockSpec((1, tm, d_model), lambda g, j: (g, j, 0))
    in_specs = [row_spec]
    args = [x]
    if with_outproj:
        width = mix[0].shape[-1]
        mix_spec = pl.BlockSpec((1, tm, width), lambda g, j: (g, j, 0))
        in_specs += [mix_spec, mix_spec]
        args += list(mix)
    in_specs.append(_ada_spec(per_row, tm, d_model))
    args.append(ada)
    if with_outproj:
        in_specs.append(_const_spec((None, w_o.shape[1], d_model), lambda g, j: (layer, 0, 0)))
        args.append(w_o)
    in_specs += [
        _const_spec((None, None, d_model, 2 * d_ff), lambda g, j: (layer, which, 0, 0)),
        _const_spec((None, None, d_ff, d_model), lambda g, j: (layer, which, 0, 0)),
        _const_spec(ln_g.shape, lambda g, j: (0, 0)),
        _const_spec(ln_b.shape, lambda g, j: (0, 0)),
    ]
    args += [w_up, w_dn, ln_g, ln_b]
    return pl.pallas_call(
        functools.partial(_ffn_kernel, sub=sub, per_row=per_row, alpha=alpha, d_ff=d_ff, ck=ck,
                          with_outproj=with_outproj),
        grid=(n_groups, rows // tm),
        in_specs=in_specs,
        out_specs=row_spec,
        out_shape=jax.ShapeDtypeStruct(x.shape, F32),
        scratch_shapes=[pltpu.VMEM((tm, d_ff), BF16)],
        compiler_params=_compiler_params(2),
        name="ffn_outproj" if with_outproj else "ffn",
    )(*args)


def _rope(z, cos, sa, sb):
    half = ROT_DIM // 2
    outs = []
    for s in range(z.shape[-1] // LANES):
        zs = z[:, s * LANES:(s + 1) * LANES]
        up = pltpu.roll(zs, LANES - half, 1)
        dn = pltpu.roll(zs, half, 1)
        outs.append(zs * cos + up * sa + dn * sb)
    return jnp.concatenate(outs, axis=-1)


def _mixer_core(x_ref, ada_ref, win_ref, wconv_ref, cos_ref, sa_ref, sb_ref, uext_scr, prefix_refs, *,
                per_row, rows_per_seq, tm, cw, qk_w):
    j = pl.program_id(1)
    x = x_ref[0]
    shift, scale, _ = _ada_vectors(ada_ref, 1, per_row)
    h = (x * (1.0 + scale) + shift).astype(BF16)

    def proj(lo, width):
        return jnp.dot(h, win_ref[:, lo:lo + width], preferred_element_type=F32)

    gb = proj(0, cw)
    gc = proj(cw, cw)
    xc = proj(2 * cw, cw)
    u = gc * xc

    @pl.when(j == 0)
    def _():
        uext_scr[0:SUBLANES, :] = jnp.zeros((SUBLANES, cw), F32)

    @pl.when(j > 0)
    def _():
        uext_scr[0:SUBLANES, :] = uext_scr[tm:tm + SUBLANES, :]

    uext_scr[SUBLANES:SUBLANES + tm, :] = u
    um2 = uext_scr[SUBLANES - 2:SUBLANES - 2 + tm, :]
    um1 = uext_scr[SUBLANES - 1:SUBLANES - 1 + tm, :]
    if per_row:
        p1_ref, p2_ref = prefix_refs
        t = lax.broadcasted_iota(jnp.int32, (tm, cw), 0) % rows_per_seq
        um2 = jnp.where(t < 2, p2_ref[0], um2)
        um1 = jnp.where(t < 1, p1_ref[0], um1)
    wc = wconv_ref[...]
    conv = um2 * wc[0:1] + um1 * wc[1:2] + u * wc[2:3]
    yc = gb * conv

    cos, sa, sb = cos_ref[...], sa_ref[...], sb_ref[...]
    q = _rope(proj(3 * cw, qk_w), cos, sa, sb) * (HEAD_DIM ** -0.5)
    k = _rope(proj(3 * cw + qk_w, qk_w), cos, sa, sb)
    v = proj(3 * cw + 2 * qk_w, qk_w)
    return yc, q, k, v


def _mixer_prompt_kernel(x_ref, ada_ref, win_ref, wconv_ref, cos_ref, sa_ref, sb_ref,
                         yc_ref, qt_ref, kb_ref, vt_ref, conv_ref, kout_ref, vout_ref, uext_scr, *,
                         tm, cw, qk_w, rows_per_seq):
    yc, q, k, v = _mixer_core(x_ref, ada_ref, win_ref, wconv_ref, cos_ref, sa_ref, sb_ref, uext_scr, None,
                              per_row=False, rows_per_seq=rows_per_seq, tm=tm, cw=cw, qk_w=qk_w)
    yc_ref[0] = yc.astype(BF16)
    conv_ref[0] = uext_scr[SUBLANES + tm - (CONV_K - 1):SUBLANES + tm, :]
    q_t = q.T
    first = (lax.broadcasted_iota(jnp.int32, q_t.shape, 0) % V_DIM) < HEAD_DIM
    qt_ref[0, 0] = jnp.where(first, q_t, 0.0).astype(BF16)
    qt_ref[0, 1] = jnp.where(first, 0.0, q_t).astype(BF16)
    kb_ref[0] = k.astype(BF16)
    kout_ref[0, 0] = k.T
    vt_ref[0] = v.T.astype(BF16)
    for h in range(N_HEADS):
        vout_ref[0, 0, pl.ds(h, tm, stride=N_HEADS), :] = v[:, h * V_DIM:(h + 1) * V_DIM]


def _mixer_prompt(x, ada, w_in, w_conv, tables, k_stack, v_stack, *, layer, depth, tm):
    batch, seq, d_model = x.shape
    cw = w_conv.shape[-1]
    n_in = w_in.shape[-1]
    qk_w = (n_in - 3 * cw) // 3
    assert qk_w == N_HEADS * V_DIM and n_in == 3 * cw + 3 * qk_w
    aliased = k_stack is not None
    tab_spec = pl.BlockSpec((tm, LANES), lambda g, j: (j, 0))
    in_specs = [
        pl.BlockSpec((1, tm, d_model), lambda g, j: (g, j, 0)),
        _ada_spec(False, tm, d_model),
        _const_spec((None, d_model, n_in), lambda g, j: (layer, 0, 0)),
        _const_spec((None, CONV_K, cw), lambda g, j: (layer, 0, 0)),
        tab_spec, tab_spec, tab_spec,
    ]
    args = [x, ada, w_in, w_conv, *tables]
    aliases = {}
    if aliased:
        in_specs += [pl.BlockSpec(memory_space=pl.ANY)] * 2
        args += [k_stack, v_stack]
        aliases = {7: 5, 8: 6}
    out_specs = [
        pl.BlockSpec((1, tm, cw), lambda g, j: (g, j, 0)),
        pl.BlockSpec((1, 2, qk_w, tm), lambda g, j: (g, 0, 0, j)),
        pl.BlockSpec((1, tm, qk_w), lambda g, j: (g, j, 0)),
        pl.BlockSpec((1, qk_w, tm), lambda g, j: (g, 0, j)),
        pl.BlockSpec((1, CONV_K - 1, cw), lambda g, j: (g, 0, 0)),
        pl.BlockSpec((1, 1, qk_w, tm), lambda g, j: (layer, g, 0, j)),
        pl.BlockSpec((1, 1, tm * N_HEADS, V_DIM), lambda g, j: (layer, g, j, 0)),
    ]
    out_shape = [
        jax.ShapeDtypeStruct((batch, seq, cw), BF16),
        jax.ShapeDtypeStruct((batch, 2, qk_w, seq), BF16),
        jax.ShapeDtypeStruct((batch, seq, qk_w), BF16),
        jax.ShapeDtypeStruct((batch, qk_w, seq), BF16),
        jax.ShapeDtypeStruct((batch, CONV_K - 1, cw), F32),
        jax.ShapeDtypeStruct((depth, batch, qk_w, seq), F32),
        jax.ShapeDtypeStruct((depth, batch, seq * N_HEADS, V_DIM), F32),
    ]
    kernel_fn = functools.partial(_mixer_prompt_kernel, tm=tm, cw=cw, qk_w=qk_w, rows_per_seq=seq)
    if aliased:
        kernel_fn = _drop_refs(kernel_fn, (7, 8))
    return pl.pallas_call(
        kernel_fn,
        grid=(batch, seq // tm),
        in_specs=in_specs,
        out_specs=out_specs,
        out_shape=out_shape,
        input_output_aliases=aliases,
        scratch_shapes=[pltpu.VMEM((tm + 2 * SUBLANES, cw), F32)],
        compiler_params=_compiler_params(2),
        name="mixer_prompt",
    )(*args)


def _drop_refs(kernel_fn, positions):
    def wrapped(*refs):
        return kernel_fn(*[r for i, r in enumerate(refs) if i not in positions])
    return wrapped


def _mixer_sample_kernel(x_ref, ada_ref, win_ref, wconv_ref, cos_ref, sa_ref, sb_ref, p1_ref, p2_ref,
                         yc_ref, qz_ref, kf_ref, vf_ref, u_ref, uext_scr, *, tm, cw, qk_w, rows_per_seq):
    yc, q, k, v = _mixer_core(x_ref, ada_ref, win_ref, wconv_ref, cos_ref, sa_ref, sb_ref, uext_scr,
                              (p1_ref, p2_ref), per_row=True, rows_per_seq=rows_per_seq, tm=tm, cw=cw, qk_w=qk_w)
    yc_ref[0] = yc.astype(BF16)
    u_ref[0] = uext_scr[SUBLANES:SUBLANES + tm, :]
    first = (lax.broadcasted_iota(jnp.int32, q.shape, 1) % V_DIM) < HEAD_DIM
    qz_ref[0, 0] = jnp.where(first, q, 0.0).astype(BF16)
    qz_ref[0, 1] = jnp.where(first, 0.0, q).astype(BF16)
    kf_ref[0] = k
    vf_ref[0] = v


def _mixer_sample(x, ada, w_in, w_conv, tables, prefix, *, layer, rows_per_seq):
    _, rows, d_model = x.shape
    cw = w_conv.shape[-1]
    n_in = w_in.shape[-1]
    qk_w = (n_in - 3 * cw) // 3
    tab_spec = pl.BlockSpec((rows, LANES), lambda g, j: (0, 0))
    w_spec = pl.BlockSpec((1, rows, cw), lambda g, j: (0, 0, 0))
    qk_spec = pl.BlockSpec((1, rows, qk_w), lambda g, j: (0, 0, 0))
    return pl.pallas_call(
        functools.partial(_mixer_sample_kernel, tm=rows, cw=cw, qk_w=qk_w, rows_per_seq=rows_per_seq),
        grid=(1, 1),
        in_specs=[
            pl.BlockSpec((1, rows, d_model), lambda g, j: (0, 0, 0)),
            _ada_spec(True, rows, d_model),
            _const_spec((None, d_model, n_in), lambda g, j: (layer, 0, 0)),
            _const_spec((None, CONV_K, cw), lambda g, j: (layer, 0, 0)),
            tab_spec, tab_spec, tab_spec, w_spec, w_spec,
        ],
        out_specs=[w_spec, pl.BlockSpec((1, 2, rows, qk_w), lambda g, j: (0, 0, 0, 0)), qk_spec, qk_spec, w_spec],
        out_shape=[
            jax.ShapeDtypeStruct((1, rows, cw), BF16),
            jax.ShapeDtypeStruct((1, 2, rows, qk_w), BF16),
            jax.ShapeDtypeStruct((1, rows, qk_w), F32),
            jax.ShapeDtypeStruct((1, rows, qk_w), F32),
            jax.ShapeDtypeStruct((1, rows, cw), F32),
        ],
        scratch_shapes=[pltpu.VMEM((rows + 2 * SUBLANES, cw), F32)],
        compiler_params=_compiler_params(2),
        name="mixer_sample",
    )(x, ada, w_in, w_conv, *tables, *prefix)


def _lambda_value(lam_ref, lambda_init):
    lf = lam_ref[...]
    a = jnp.sum(lf[0:1] * lf[1:2], axis=-1, keepdims=True)
    b = jnp.sum(lf[2:3] * lf[3:4], axis=-1, keepdims=True)
    return jnp.exp(a) - jnp.exp(b) + lambda_init


def _sub_layer_norm(o, g, lambda_init):
    o = o * lax.rsqrt(jnp.mean(o * o, axis=-1, keepdims=True) + RMS_EPS) * g
    return o * (1.0 - lambda_init)


def _attn_kernel(qt_ref, k_ref, vt_ref, lam_ref, g_ref, o_ref, sa_scr, sb_scr, m_scr, l_scr, acc_scr, *,
                 tq, lambda_init):
    qi = pl.program_id(1)
    m_scr[...] = jnp.full(m_scr.shape, -jnp.inf, F32)
    l_scr[...] = jnp.zeros(l_scr.shape, F32)
    acc_scr[...] = jnp.zeros(acc_scr.shape, F32)

    def scores(j, s_scr):
        start = pl.multiple_of(j * tq, tq)
        for h in range(N_HEADS):
            hs = slice(h * V_DIM, (h + 1) * V_DIM)
            q_t = jnp.concatenate([qt_ref[0, 0, hs, :], qt_ref[0, 1, hs, :]], axis=-1)
            s_scr[h] = jnp.dot(k_ref[0, pl.ds(start, tq), hs], q_t, preferred_element_type=F32)

    def consume(j, s_scr, masked):
        start = pl.multiple_of(j * tq, tq)
        for h in range(N_HEADS):
            s = s_scr[h]
            if masked:
                key = lax.broadcasted_iota(jnp.int32, s.shape, 0)
                qry = lax.broadcasted_iota(jnp.int32, s.shape, 1) % tq
                s = jnp.where(key <= qry, s, -jnp.inf)
            m_prev = m_scr[h:h + 1]
            m_next = jnp.maximum(m_prev, jnp.max(s, axis=0, keepdims=True))
            p = jnp.exp(s - m_next)
            alpha = jnp.exp(m_prev - m_next)
            m_scr[h:h + 1] = m_next
            l_scr[h:h + 1] = alpha * l_scr[h:h + 1] + jnp.sum(p, axis=0, keepdims=True)
            v_t = vt_ref[0, h * V_DIM:(h + 1) * V_DIM, pl.ds(start, tq)]
            acc_scr[h] = alpha * acc_scr[h] + jnp.dot(v_t, p.astype(BF16), preferred_element_type=F32)

    scores(0, sa_scr)

    def pair(jj, carry):
        scores(2 * jj + 1, sb_scr)
        consume(2 * jj, sa_scr, False)
        scores(2 * jj + 2, sa_scr)
        consume(2 * jj + 1, sb_scr, False)
        return carry

    lax.fori_loop(0, qi // 2, pair, 0)

    @pl.when(qi % 2 == 0)
    def _():
        consume(qi, sa_scr, True)

    @pl.when(qi % 2 == 1)
    def _():
        scores(qi, sb_scr)
        consume(qi - 1, sa_scr, False)
        consume(qi, sb_scr, True)

    lam = _lambda_value(lam_ref, lambda_init)
    for h in range(N_HEADS):
        a = acc_scr[h] * (1.0 / l_scr[h:h + 1])
        o_t = a[:, :tq] - lam * a[:, tq:]
        o_ref[0, :, h * V_DIM:(h + 1) * V_DIM] = _sub_layer_norm(o_t.T, g_ref[...], lambda_init).astype(o_ref.dtype)


def _prompt_attention(q_t, k, v_t, lam_qk, subln_g, *, lambda_init, tq):
    batch, _, width, seq = q_t.shape
    assert width == N_HEADS * V_DIM
    return pl.pallas_call(
        functools.partial(_attn_kernel, tq=tq, lambda_init=lambda_init),
        grid=(batch, seq // tq),
        in_specs=[
            pl.BlockSpec((1, 2, width, tq), lambda b, i: (b, 0, 0, i)),
            pl.BlockSpec((1, seq, width), lambda b, i: (b, 0, 0)),
            pl.BlockSpec((1, width, seq), lambda b, i: (b, 0, 0)),
            pl.BlockSpec(lam_qk.shape, lambda b, i: (0, 0)),
            pl.BlockSpec(subln_g.shape, lambda b, i: (0, 0)),
        ],
        out_specs=pl.BlockSpec((1, tq, width), lambda b, i: (b, i, 0)),
        out_shape=jax.ShapeDtypeStruct((batch, seq, width), BF16),
        scratch_shapes=[
            pltpu.VMEM((N_HEADS, tq, 2 * tq), F32),
            pltpu.VMEM((N_HEADS, tq, 2 * tq), F32),
            pltpu.VMEM((SUBLANES, 2 * tq), F32),
            pltpu.VMEM((SUBLANES, 2 * tq), F32),
            pltpu.VMEM((N_HEADS, V_DIM, 2 * tq), F32),
        ],
        compiler_params=_compiler_params(2),
        name="prompt_attention",
    )(q_t, k, v_t, lam_qk, subln_g)


def _decode_kernel(pt_ref, q_ref, *refs, pages_per_step, page, n_new, lambda_init):
    del pt_ref
    npg = pages_per_step
    k_refs, v_refs = refs[:npg], refs[npg:2 * npg]
    knew_ref, vnew_ref, lam_ref, g_ref, o_ref, kbf_scr, vbf_scr, m_scr, l_scr, acc_scr = refs[2 * npg:]
    p = pl.program_id(1)
    rows = q_ref.shape[2]

    @pl.when(p == 0)
    def _():
        m_scr[...] = jnp.full(m_scr.shape, -jnp.inf, F32)
        l_scr[...] = jnp.zeros(l_scr.shape, F32)
        acc_scr[...] = jnp.zeros(acc_scr.shape, F32)

    def update(k_t, v, mask=None):
        s = jnp.concatenate(
            [jnp.dot(q_ref[0, h].astype(BF16), k_t(h), preferred_element_type=F32) for h in range(N_HEADS)], axis=0)
        if mask is not None:
            s = jnp.where(mask, s, -jnp.inf)
        n_rep = s.shape[-1] // LANES
        m_prev = m_scr[...]
        m_next = jnp.maximum(m_prev, jnp.max(s, axis=-1, keepdims=True))
        pr = jnp.exp(s - jnp.concatenate([m_next] * n_rep, axis=-1))
        alpha = jnp.exp(m_prev - m_next)
        pv = jnp.concatenate(
            [jnp.dot(pr[h * rows:(h + 1) * rows].astype(BF16), v(h), preferred_element_type=F32)
             for h in range(N_HEADS)], axis=0)
        m_scr[...] = m_next
        l_scr[...] = alpha * l_scr[...] + jnp.sum(pr, axis=-1, keepdims=True)
        acc_scr[...] = alpha * acc_scr[...] + pv

    for i in range(npg):
        kbf_scr[:, i * page:(i + 1) * page] = k_refs[i][0, 0].astype(BF16)
        for h in range(N_HEADS):
            vbf_scr[h, i * page:(i + 1) * page, :] = (
                v_refs[i][0, 0, pl.ds(h, page, stride=N_HEADS), :].astype(BF16))
    update(lambda h: kbf_scr[h * V_DIM:(h + 1) * V_DIM, :], lambda h: vbf_scr[h])

    @pl.when(p == pl.num_programs(1) - 1)
    def _():
        row_t = lax.broadcasted_iota(jnp.int32, (N_HEADS * rows, LANES), 0) % n_new
        col = lax.broadcasted_iota(jnp.int32, (N_HEADS * rows, LANES), 1)
        update(lambda h: knew_ref[0, h * V_DIM:(h + 1) * V_DIM, :].astype(BF16),
               lambda h: vnew_ref[0, :, h * V_DIM:(h + 1) * V_DIM].astype(BF16), col <= row_t)
        lam = _lambda_value(lam_ref, lambda_init)
        a = acc_scr[...] / l_scr[...]
        for h in range(N_HEADS):
            ah = a[h * rows:(h + 1) * rows]
            o = ah - lam * pltpu.roll(ah, rows - n_new, 0)
            o_ref[0, h] = _sub_layer_norm(o, g_ref[...], lambda_init)


def _decode_attention(qd, cache_k, cache_v, page_table, k_new, v_new, lam_qk, subln_g, *, layer, lambda_init,
                      pages_per_step):
    n_seq, n_heads, rows, _ = qd.shape
    n_new = rows // 2
    _, _, width, page = cache_k.shape
    n_pages = page_table.shape[1]
    npg = pages_per_step
    assert n_pages % npg == 0 and n_heads == N_HEADS and rows % SUBLANES == 0 and page == LANES

    def k_spec(i):
        return pl.BlockSpec((1, 1, width, page), lambda n, p, pt: (layer, pt[n, p * npg + i], 0, 0))

    def v_spec(i):
        return pl.BlockSpec((1, 1, page * n_heads, V_DIM), lambda n, p, pt: (layer, pt[n, p * npg + i], 0, 0))

    q_spec = pl.BlockSpec((1, n_heads, rows, V_DIM), lambda n, p, pt: (n, 0, 0, 0))
    grid_spec = pltpu.PrefetchScalarGridSpec(
        num_scalar_prefetch=1,
        grid=(n_seq, n_pages // npg),
        in_specs=[q_spec] + [k_spec(i) for i in range(npg)] + [v_spec(i) for i in range(npg)] + [
            pl.BlockSpec((1, width, LANES), lambda n, p, pt: (n, 0, 0)),
            pl.BlockSpec((1, LANES, width), lambda n, p, pt: (n, 0, 0)),
            pl.BlockSpec(lam_qk.shape, lambda n, p, pt: (0, 0)),
            pl.BlockSpec(subln_g.shape, lambda n, p, pt: (0, 0)),
        ],
        out_specs=q_spec,
        scratch_shapes=[pltpu.VMEM((width, npg * page), BF16), pltpu.VMEM((n_heads, npg * page, V_DIM), BF16)]
        + [pltpu.VMEM((n_heads * rows, LANES), F32)] * 3,
    )
    return pl.pallas_call(
        functools.partial(_decode_kernel, pages_per_step=npg, page=page, n_new=n_new, lambda_init=lambda_init),
        grid_spec=grid_spec,
        out_shape=jax.ShapeDtypeStruct(qd.shape, F32),
        compiler_params=_compiler_params(2),
        name="decode_attention",
    )(page_table, qd, *([cache_k] * npg), *([cache_v] * npg), k_new, v_new, lam_qk, subln_g)


def _pick_tile(rows, target):
    tile = min(rows, target)
    while rows % tile:
        tile //= 2
    return tile


def kernel(x_prompt, x_sample, cache_k, cache_v, state_conv, page_table, c_prompt, c_sample, w_ada, b_ada, ln_g, ln_b, ffn_w_up, ffn_w_down, w_in, w_conv, lambda_qk, subln_g, w_o):
    depth = w_ada.shape[0]
    batch, seq, d_model = x_prompt.shape
    n_seq, n_new, _ = x_sample.shape
    _, n_pool, page, n_heads, _, head_dim = cache_k.shape
    assert n_heads == N_HEADS and head_dim == HEAD_DIM and page == LANES
    cw = w_conv.shape[-1]
    qk_w = N_HEADS * V_DIM
    past_len = page_table.shape[1] * page
    alpha = (2.0 * depth) ** 0.25
    s_rows = n_seq * n_new

    w_up_b = ffn_w_up.astype(BF16)
    w_dn_b = ffn_w_down.astype(BF16)
    w_in_b = w_in.astype(BF16)
    w_o_b = w_o.astype(BF16)
    cache_k2 = jnp.transpose(cache_k, (0, 1, 3, 4, 5, 2)).reshape(depth, n_pool, qk_w, page)
    cache_v2 = cache_v.reshape(depth, n_pool, page * N_HEADS, V_DIM)

    ada_all = _adaln(jnp.concatenate([c_prompt, c_sample], axis=0), w_ada, b_ada)
    ada_all = ada_all.reshape(depth, batch + n_seq, 3 * N_SUB, d_model)

    tab_p = _rope_tables(jnp.arange(seq, dtype=F32))
    tab_s = _rope_tables(past_len + (jnp.arange(s_rows, dtype=jnp.int32) % n_new).astype(F32))

    tm_p = _pick_tile(seq, 512)
    tq = _pick_tile(seq, 256)
    pages_per_step = _pick_tile(page_table.shape[1], 16)

    xp = x_prompt
    xs = x_sample.reshape(1, s_rows, d_model)
    k_stack = v_stack = None
    outs = {k: [] for k in ("cp", "ks", "vs", "cs")}
    for l in range(depth):
        lambda_init = 0.8 - 0.6 * math.exp(-0.3 * l)
        lam_l = lambda_qk[l]
        g_l = subln_g[l].reshape(1, V_DIM)
        ln_g0, ln_b0 = ln_g[l, 0:1], ln_b[l, 0:1]
        ln_g12, ln_b12 = ln_g[l, 1:3], ln_b[l, 1:3]

        ada_p = ada_all[l, :batch]
        xp = _ffn(xp, ada_p, w_up_b, w_dn_b, ln_g0, ln_b0, layer=l, which=0, sub=0, per_row=False,
                  alpha=alpha, tm=tm_p)
        yc, q_t, kb, v_t, conv, k_stack, v_stack = _mixer_prompt(
            xp, ada_p, w_in_b, w_conv, tab_p, k_stack, v_stack, layer=l, depth=depth, tm=tm_p)
        o = _prompt_attention(q_t, kb, v_t, lam_l, g_l, lambda_init=lambda_init, tq=tq)
        xp = _ffn(xp, ada_p, w_up_b, w_dn_b, ln_g12, ln_b12, layer=l, which=1, sub=2, per_row=False,
                  alpha=alpha, tm=tm_p, mix=(yc, o), w_o=w_o_b)
        outs["cp"].append(conv)

        ada_s = jnp.repeat(jnp.transpose(ada_all[l, batch:], (1, 0, 2)), n_new, axis=1)
        pre = state_conv[l]
        zeros = jnp.zeros((n_seq, n_new - 1, cw), F32)
        p1 = jnp.concatenate([pre[:, 1:2], zeros], axis=1).reshape(1, s_rows, cw)
        p2 = jnp.concatenate([pre, zeros[:, 1:]], axis=1).reshape(1, s_rows, cw)
        xs = _ffn(xs, ada_s, w_up_b, w_dn_b, ln_g0, ln_b0, layer=l, which=0, sub=0, per_row=True,
                  alpha=alpha, tm=s_rows)
        yc, qz, kf, vf, u_s = _mixer_sample(xs, ada_s, w_in_b, w_conv, tab_s, (p1, p2), layer=l, rows_per_seq=n_new)
        qd = qz.astype(F32).reshape(2, n_seq, n_new, N_HEADS, V_DIM)
        qd = jnp.transpose(qd, (1, 3, 0, 2, 4)).reshape(n_seq, N_HEADS, 2 * n_new, V_DIM)
        pad = ((0, 0), (0, LANES - n_new), (0, 0))
        k_new = jnp.transpose(jnp.pad(kf.reshape(n_seq, n_new, qk_w), pad), (0, 2, 1))
        v_new = jnp.pad(vf.reshape(n_seq, n_new, qk_w), pad)
        od = _decode_attention(qd, cache_k2, cache_v2, page_table, k_new, v_new, lam_l, g_l, layer=l,
                               lambda_init=lambda_init, pages_per_step=pages_per_step)
        o = jnp.transpose(od[:, :, :n_new], (0, 2, 1, 3)).reshape(1, s_rows, qk_w).astype(BF16)
        xs = _ffn(xs, ada_s, w_up_b, w_dn_b, ln_g12, ln_b12, layer=l, which=1, sub=2, per_row=True,
                  alpha=alpha, tm=s_rows, mix=(yc, o), w_o=w_o_b)
        outs["ks"].append(kf.reshape(n_seq, n_new, N_HEADS, 2, HEAD_DIM))
        outs["vs"].append(vf.reshape(n_seq, n_new, N_HEADS, V_DIM))
        outs["cs"].append(u_s.reshape(n_seq, n_new, cw)[:, n_new - (CONV_K - 1):])

    k_prompt = jnp.transpose(k_stack.reshape(depth, batch, N_HEADS, 2, HEAD_DIM, seq), (0, 1, 5, 2, 3, 4))
    v_prompt = v_stack.reshape(depth, batch, seq, N_HEADS, V_DIM)
    return (xp, xs.reshape(n_seq, n_new, d_model), k_prompt, v_prompt, jnp.stack(outs["cp"]),
            jnp.stack(outs["ks"]), jnp.stack(outs["vs"]), jnp.stack(outs["cs"]))
```

```python
import functools
import math

import jax
import jax.numpy as jnp
from jax import lax
from jax.experimental import pallas as pl
from jax.experimental.pallas import tpu as pltpu

N_HEADS = 4
HEAD_DIM = 64
V_DIM = 2 * HEAD_DIM
CONV_K = 3
ROT_DIM = HEAD_DIM // 4
ROPE_THETA = 500000.0
LN_EPS = 1e-5
RMS_EPS = 1e-5
N_SUB = 3

LANES = 128
SUBLANES = 8
V7X_VMEM_BYTES = 64 * 1024 * 1024
VMEM_LIMIT_BYTES = V7X_VMEM_BYTES - 8 * 1024 * 1024

F32 = jnp.float32
BF16 = jnp.bfloat16


def _compiler_params(n_grid_dims):
    return pltpu.CompilerParams(
        dimension_semantics=("arbitrary",) * n_grid_dims,
        vmem_limit_bytes=VMEM_LIMIT_BYTES,
    )


def _const_spec(shape, index_map):
    return pl.BlockSpec(shape, index_map, pipeline_mode=pl.Buffered(1))


def _layer_norm(y, g, b):
    mu = jnp.mean(y, axis=-1, keepdims=True)
    d = y - mu
    var = jnp.mean(d * d, axis=-1, keepdims=True)
    return d * lax.rsqrt(var + LN_EPS) * g + b


def _ada_vectors(ada_ref, sub, per_row):
    if per_row:
        return ada_ref[3 * sub], ada_ref[3 * sub + 1], ada_ref[3 * sub + 2]
    a = ada_ref[0]
    return a[3 * sub:3 * sub + 1], a[3 * sub + 1:3 * sub + 2], a[3 * sub + 2:3 * sub + 3]


def _ada_spec(per_row, tm, d_model):
    if per_row:
        return pl.BlockSpec((3 * N_SUB, tm, d_model), lambda g, j: (0, j, 0))
    return pl.BlockSpec((1, 3 * N_SUB, d_model), lambda g, j: (g, 0, 0))


def _adaln_kernel(c_ref, w_ref, b_ref, o_ref):
    c = c_ref[...]
    h = (c * jax.nn.sigmoid(c)).astype(BF16)
    o_ref[0] = jnp.dot(h, w_ref[0].astype(BF16), preferred_element_type=F32) + b_ref[0]


def _adaln(c_all, w_ada, b_ada):
    depth, d_model, n_out = w_ada.shape
    rows = c_all.shape[0]
    tn = 1024 if n_out % 1024 == 0 else n_out
    return pl.pallas_call(
        _adaln_kernel,
        grid=(depth, n_out // tn),
        in_specs=[
            pl.BlockSpec((rows, d_model), lambda l, n: (0, 0)),
            pl.BlockSpec((1, d_model, tn), lambda l, n: (l, 0, n)),
            pl.BlockSpec((1, 1, tn), lambda l, n: (l, 0, n)),
        ],
        out_specs=pl.BlockSpec((1, rows, tn), lambda l, n: (l, 0, n)),
        out_shape=jax.ShapeDtypeStruct((depth, rows, n_out), F32),
        compiler_params=_compiler_params(2),
        name="adaln",
    )(c_all, w_ada, b_ada.reshape(depth, 1, n_out))


def _rope_table_kernel(pos_ref, inv_ref, cos_ref, sa_ref, sb_ref):
    ang = pos_ref[...] * inv_ref[...]
    lane = lax.broadcasted_iota(jnp.int32, ang.shape, 1) % HEAD_DIM
    half = ROT_DIM // 2
    c = jnp.cos(ang)
    s = jnp.sin(ang)
    cos_ref[...] = jnp.where(lane < ROT_DIM, c, 1.0)
    sa_ref[...] = jnp.where(lane < half, -s, 0.0)
    sb_ref[...] = jnp.where((lane >= half) & (lane < ROT_DIM), s, 0.0)


def _rope_tables(pos):
    rows = pos.shape[0]
    half = ROT_DIM // 2
    lane = jnp.arange(LANES, dtype=jnp.int32) % HEAD_DIM % half
    inv_freq = jnp.power(ROPE_THETA, -(2 * lane).astype(F32) / ROT_DIM).reshape(1, LANES)
    shp = jax.ShapeDtypeStruct((rows, LANES), F32)
    return pl.pallas_call(
        _rope_table_kernel,
        out_shape=(shp, shp, shp),
        name="rope_tables",
    )(pos.reshape(rows, 1), inv_freq)


def _ffn_kernel(*refs, sub, per_row, alpha, d_ff, ck, with_outproj):
    if with_outproj:
        (x_ref, yc_ref, o_ref_in, ada_ref, wo_ref, wup_ref, wdn_ref, lng_ref, lnb_ref,
         out_ref, hid_scr) = refs
    else:
        x_ref, ada_ref, wup_ref, wdn_ref, lng_ref, lnb_ref, out_ref, hid_scr = refs
    x = x_ref[0]
    if with_outproj:
        _, _, gate1 = _ada_vectors(ada_ref, 1, per_row)
        mix = jnp.concatenate([yc_ref[0], o_ref_in[0]], axis=-1)
        y = jnp.dot(mix, wo_ref[...], preferred_element_type=F32)
        x = _layer_norm(alpha * x + (1.0 + gate1) * y, lng_ref[0:1], lnb_ref[0:1])
        ln_row = 1
    else:
        ln_row = 0
    shift, scale, gate = _ada_vectors(ada_ref, sub, per_row)
    h = (x * (1.0 + scale) + shift).astype(BF16)
    for j in range(d_ff // ck):
        g = jnp.dot(h, wup_ref[:, j * ck:(j + 1) * ck], preferred_element_type=F32)
        u = jnp.dot(h, wup_ref[:, d_ff + j * ck:d_ff + (j + 1) * ck], preferred_element_type=F32)
        hid_scr[:, j * ck:(j + 1) * ck] = (g * jax.nn.sigmoid(g) * u).astype(BF16)
    f = jnp.dot(hid_scr[...], wdn_ref[...], preferred_element_type=F32)
    y = alpha * x + (1.0 + gate) * (0.5 * f)
    out_ref[0] = _layer_norm(y, lng_ref[ln_row:ln_row + 1], lnb_ref[ln_row:ln_row + 1])


def _ffn(x, ada, w_up, w_dn, ln_g, ln_b, *, layer, which, sub, per_row, alpha, tm, mix=None, w_o=None):
    n_groups, rows, d_model = x.shape
    d_ff = w_dn.shape[2]
    ck = 256 if d_ff % 256 == 0 else d_ff
    with_outproj = mix is not None
    row_spec = pl.BlockSpec((1, tm, d_model), lambda g, j: (g, j, 0))
    in_specs = [row_spec]
    args = [x]
    if with_outproj:
        width = mix[0].shape[-1]
        mix_spec = pl.BlockSpec((1, tm, width), lambda g, j: (g, j, 0))
        in_specs += [mix_spec, mix_spec]
        args += list(mix)
    in_specs.append(_ada_spec(per_row, tm, d_model))
    args.append(ada)
    if with_outproj:
        in_specs.append(_const_spec((None, w_o.shape[1], d_model), lambda g, j: (layer, 0, 0)))
        args.append(w_o)
    in_specs += [
        _const_spec((None, None, d_model, 2 * d_ff), lambda g, j: (layer, which, 0, 0)),
        _const_spec((None, None, d_ff, d_model), lambda g, j: (layer, which, 0, 0)),
        _const_spec(ln_g.shape, lambda g, j: (0, 0)),
        _const_spec(ln_b.shape, lambda g, j: (0, 0)),
    ]
    args += [w_up, w_dn, ln_g, ln_b]
    return pl.pallas_call(
        functools.partial(_ffn_kernel, sub=sub, per_row=per_row, alpha=alpha, d_ff=d_ff, ck=ck,
                          with_outproj=with_outproj),
        grid=(n_groups, rows // tm),
        in_specs=in_specs,
        out_specs=row_spec,
        out_shape=jax.ShapeDtypeStruct(x.shape, F32),
        scratch_shapes=[pltpu.VMEM((tm, d_ff), BF16)],
        compiler_params=_compiler_params(2),
        name="ffn_outproj" if with_outproj else "ffn",
    )(*args)


def _rope(z, cos, sa, sb):
    half = ROT_DIM // 2
    outs = []
    for s in range(z.shape[-1] // LANES):
        zs = z[:, s * LANES:(s + 1) * LANES]
        up = pltpu.roll(zs, LANES - half, 1)
        dn = pltpu.roll(zs, half, 1)
        outs.append(zs * cos + up * sa + dn * sb)
    return jnp.concatenate(outs, axis=-1)


def _mixer_core(x_ref, ada_ref, win_ref, wconv_ref, cos_ref, sa_ref, sb_ref, uext_scr, prefix_refs, *,
                per_row, rows_per_seq, tm, cw, qk_w):
    j = pl.program_id(1)
    x = x_ref[0]
    shift, scale, _ = _ada_vectors(ada_ref, 1, per_row)
    h = (x * (1.0 + scale) + shift).astype(BF16)

    def proj(lo, width):
        return jnp.dot(h, win_ref[:, lo:lo + width], preferred_element_type=F32)

    gb = proj(0, cw)
    gc = proj(cw, cw)
    xc = proj(2 * cw, cw)
    u = gc * xc

    @pl.when(j == 0)
    def _():
        uext_scr[0:SUBLANES, :] = jnp.zeros((SUBLANES, cw), F32)

    @pl.when(j > 0)
    def _():
        uext_scr[0:SUBLANES, :] = uext_scr[tm:tm + SUBLANES, :]

    uext_scr[SUBLANES:SUBLANES + tm, :] = u
    um2 = uext_scr[SUBLANES - 2:SUBLANES - 2 + tm, :]
    um1 = uext_scr[SUBLANES - 1:SUBLANES - 1 + tm, :]
    if per_row:
        p1_ref, p2_ref = prefix_refs
        t = lax.broadcasted_iota(jnp.int32, (tm, cw), 0) % rows_per_seq
        um2 = jnp.where(t < 2, p2_ref[0], um2)
        um1 = jnp.where(t < 1, p1_ref[0], um1)
    wc = wconv_ref[...]
    conv = um2 * wc[0:1] + um1 * wc[1:2] + u * wc[2:3]
    yc = gb * conv

    cos, sa, sb = cos_ref[...], sa_ref[...], sb_ref[...]
    q = _rope(proj(3 * cw, qk_w), cos, sa, sb) * (HEAD_DIM ** -0.5)
    k = _rope(proj(3 * cw + qk_w, qk_w), cos, sa, sb)
    v = proj(3 * cw + 2 * qk_w, qk_w)
    return yc, q, k, v


def _mixer_prompt_kernel(x_ref, ada_ref, win_ref, wconv_ref, cos_ref, sa_ref, sb_ref,
                         yc_ref, qt_ref, kb_ref, vt_ref, conv_ref, kout_ref, vout_ref, uext_scr, *,
                         tm, cw, qk_w, rows_per_seq):
    yc, q, k, v = _mixer_core(x_ref, ada_ref, win_ref, wconv_ref, cos_ref, sa_ref, sb_ref, uext_scr, None,
                              per_row=False, rows_per_seq=rows_per_seq, tm=tm, cw=cw, qk_w=qk_w)
    yc_ref[0] = yc.astype(BF16)
    conv_ref[0] = uext_scr[SUBLANES + tm - (CONV_K - 1):SUBLANES + tm, :]
    q_t = q.T
    first = (lax.broadcasted_iota(jnp.int32, q_t.shape, 0) % V_DIM) < HEAD_DIM
    qt_ref[0, 0] = jnp.where(first, q_t, 0.0).astype(BF16)
    qt_ref[0, 1] = jnp.where(first, 0.0, q_t).astype(BF16)
    kb_ref[0] = k.astype(BF16)
    kout_ref[0, 0] = k.T
    vt_ref[0] = v.T.astype(BF16)
    for h in range(N_HEADS):
        vout_ref[0, 0, pl.ds(h, tm, stride=N_HEADS), :] = v[:, h * V_DIM:(h + 1) * V_DIM]


def _mixer_prompt(x, ada, w_in, w_conv, tables, k_stack, v_stack, *, layer, depth, tm):
    batch, seq, d_model = x.shape
    cw = w_conv.shape[-1]
    n_in = w_in.shape[-1]
    qk_w = (n_in - 3 * cw) // 3
    assert qk_w == N_HEADS * V_DIM and n_in == 3 * cw + 3 * qk_w
    aliased = k_stack is not None
    tab_spec = pl.BlockSpec((tm, LANES), lambda g, j: (j, 0))
    in_specs = [
        pl.BlockSpec((1, tm, d_model), lambda g, j: (g, j, 0)),
        _ada_spec(False, tm, d_model),
        _const_spec((None, d_model, n_in), lambda g, j: (layer, 0, 0)),
        _const_spec((None, CONV_K, cw), lambda g, j: (layer, 0, 0)),
        tab_spec, tab_spec, tab_spec,
    ]
    args = [x, ada, w_in, w_conv, *tables]
    aliases = {}
    if aliased:
        in_specs += [pl.BlockSpec(memory_space=pl.ANY)] * 2
        args += [k_stack, v_stack]
        aliases = {7: 5, 8: 6}
    out_specs = [
        pl.BlockSpec((1, tm, cw), lambda g, j: (g, j, 0)),
        pl.BlockSpec((1, 2, qk_w, tm), lambda g, j: (g, 0, 0, j)),
        pl.BlockSpec((1, tm, qk_w), lambda g, j: (g, j, 0)),
        pl.BlockSpec((1, qk_w, tm), lambda g, j: (g, 0, j)),
        pl.BlockSpec((1, CONV_K - 1, cw), lambda g, j: (g, 0, 0)),
        pl.BlockSpec((1, 1, qk_w, tm), lambda g, j: (layer, g, 0, j)),
        pl.BlockSpec((1, 1, tm * N_HEADS, V_DIM), lambda g, j: (layer, g, j, 0)),
    ]
    out_shape = [
        jax.ShapeDtypeStruct((batch, seq, cw), BF16),
        jax.ShapeDtypeStruct((batch, 2, qk_w, seq), BF16),
        jax.ShapeDtypeStruct((batch, seq, qk_w), BF16),
        jax.ShapeDtypeStruct((batch, qk_w, seq), BF16),
        jax.ShapeDtypeStruct((batch, CONV_K - 1, cw), F32),
        jax.ShapeDtypeStruct((depth, batch, qk_w, seq), F32),
        jax.ShapeDtypeStruct((depth, batch, seq * N_HEADS, V_DIM), F32),
    ]
    kernel_fn = functools.partial(_mixer_prompt_kernel, tm=tm, cw=cw, qk_w=qk_w, rows_per_seq=seq)
    if aliased:
        kernel_fn = _drop_refs(kernel_fn, (7, 8))
    return pl.pallas_call(
        kernel_fn,
        grid=(batch, seq // tm),
        in_specs=in_specs,
        out_specs=out_specs,
        out_shape=out_shape,
        input_output_aliases=aliases,
        scratch_shapes=[pltpu.VMEM((tm + 2 * SUBLANES, cw), F32)],
        compiler_params=_compiler_params(2),
        name="mixer_prompt",
    )(*args)


def _drop_refs(kernel_fn, positions):
    def wrapped(*refs):
        return kernel_fn(*[r for i, r in enumerate(refs) if i not in positions])
    return wrapped


def _mixer_sample_kernel(x_ref, ada_ref, win_ref, wconv_ref, cos_ref, sa_ref, sb_ref, p1_ref, p2_ref,
                         yc_ref, qz_ref, kf_ref, vf_ref, u_ref, uext_scr, *, tm, cw, qk_w, rows_per_seq):
    yc, q, k, v = _mixer_core(x_ref, ada_ref, win_ref, wconv_ref, cos_ref, sa_ref, sb_ref, uext_scr,
                              (p1_ref, p2_ref), per_row=True, rows_per_seq=rows_per_seq, tm=tm, cw=cw, qk_w=qk_w)
    yc_ref[0] = yc.astype(BF16)
    u_ref[0] = uext_scr[SUBLANES:SUBLANES + tm, :]
    first = (lax.broadcasted_iota(jnp.int32, q.shape, 1) % V_DIM) < HEAD_DIM
    qz_ref[0, 0] = jnp.where(first, q, 0.0).astype(BF16)
    qz_ref[0, 1] = jnp.where(first, 0.0, q).astype(BF16)
    kf_ref[0] = k
    vf_ref[0] = v


def _mixer_sample(x, ada, w_in, w_conv, tables, prefix, *, layer, rows_per_seq):
    _, rows, d_model = x.shape
    cw = w_conv.shape[-1]
    n_in = w_in.shape[-1]
    qk_w = (n_in - 3 * cw) // 3
    tab_spec = pl.BlockSpec((rows, LANES), lambda g, j: (0, 0))
    w_spec = pl.BlockSpec((1, rows, cw), lambda g, j: (0, 0, 0))
    qk_spec = pl.BlockSpec((1, rows, qk_w), lambda g, j: (0, 0, 0))
    return pl.pallas_call(
        functools.partial(_mixer_sample_kernel, tm=rows, cw=cw, qk_w=qk_w, rows_per_seq=rows_per_seq),
        grid=(1, 1),
        in_specs=[
            pl.BlockSpec((1, rows, d_model), lambda g, j: (0, 0, 0)),
            _ada_spec(True, rows, d_model),
            _const_spec((None, d_model, n_in), lambda g, j: (layer, 0, 0)),
            _const_spec((None, CONV_K, cw), lambda g, j: (layer, 0, 0)),
            tab_spec, tab_spec, tab_spec, w_spec, w_spec,
        ],
        out_specs=[w_spec, pl.BlockSpec((1, 2, rows, qk_w), lambda g, j: (0, 0, 0, 0)), qk_spec, qk_spec, w_spec],
        out_shape=[
            jax.ShapeDtypeStruct((1, rows, cw), BF16),
            jax.ShapeDtypeStruct((1, 2, rows, qk_w), BF16),
            jax.ShapeDtypeStruct((1, rows, qk_w), F32),
            jax.ShapeDtypeStruct((1, rows, qk_w), F32),
            jax.ShapeDtypeStruct((1, rows, cw), F32),
        ],
        scratch_shapes=[pltpu.VMEM((rows + 2 * SUBLANES, cw), F32)],
        compiler_params=_compiler_params(2),
        name="mixer_sample",
    )(x, ada, w_in, w_conv, *tables, *prefix)


def _lambda_value(lam_ref, lambda_init):
    lf = lam_ref[...]
    a = jnp.sum(lf[0:1] * lf[1:2], axis=-1, keepdims=True)
    b = jnp.sum(lf[2:3] * lf[3:4], axis=-1, keepdims=True)
    return jnp.exp(a) - jnp.exp(b) + lambda_init


def _sub_layer_norm(o, g, lambda_init):
    o = o * lax.rsqrt(jnp.mean(o * o, axis=-1, keepdims=True) + RMS_EPS) * g
    return o * (1.0 - lambda_init)


def _attn_kernel(qt_ref, k_ref, vt_ref, lam_ref, g_ref, o_ref, sa_scr, sb_scr, m_scr, l_scr, acc_scr, *,
                 tq, lambda_init):
    qi = pl.program_id(1)
    m_scr[...] = jnp.full(m_scr.shape, -jnp.inf, F32)
    l_scr[...] = jnp.zeros(l_scr.shape, F32)
    acc_scr[...] = jnp.zeros(acc_scr.shape, F32)

    def scores(j, s_scr):
        start = pl.multiple_of(j * tq, tq)
        for h in range(N_HEADS):
            hs = slice(h * V_DIM, (h + 1) * V_DIM)
            q_t = jnp.concatenate([qt_ref[0, 0, hs, :], qt_ref[0, 1, hs, :]], axis=-1)
            s_scr[h] = jnp.dot(k_ref[0, pl.ds(start, tq), hs], q_t, preferred_element_type=F32)

    def consume(j, s_scr, masked):
        start = pl.multiple_of(j * tq, tq)
        for h in range(N_HEADS):
            s = s_scr[h]
            if masked:
                key = lax.broadcasted_iota(jnp.int32, s.shape, 0)
                qry = lax.broadcasted_iota(jnp.int32, s.shape, 1) % tq
                s = jnp.where(key <= qry, s, -jnp.inf)
            m_prev = m_scr[h:h + 1]
            m_next = jnp.maximum(m_prev, jnp.max(s, axis=0, keepdims=True))
            p = jnp.exp(s - m_next)
            alpha = jnp.exp(m_prev - m_next)
            m_scr[h:h + 1] = m_next
            l_scr[h:h + 1] = alpha * l_scr[h:h + 1] + jnp.sum(p, axis=0, keepdims=True)
            v_t = vt_ref[0, h * V_DIM:(h + 1) * V_DIM, pl.ds(start, tq)]
            acc_scr[h] = alpha * acc_scr[h] + jnp.dot(v_t, p.astype(BF16), preferred_element_type=F32)

    scores(0, sa_scr)

    def pair(jj, carry):
        scores(2 * jj + 1, sb_scr)
        consume(2 * jj, sa_scr, False)
        scores(2 * jj + 2, sa_scr)
        consume(2 * jj + 1, sb_scr, False)
        return carry

    lax.fori_loop(0, qi // 2, pair, 0)

    @pl.when(qi % 2 == 0)
    def _():
        consume(qi, sa_scr, True)

    @pl.when(qi % 2 == 1)
    def _():
        scores(qi, sb_scr)
        consume(qi - 1, sa_scr, False)
        consume(qi, sb_scr, True)

    lam = _lambda_value(lam_ref, lambda_init)
    for h in range(N_HEADS):
        a = acc_scr[h] * (1.0 / l_scr[h:h + 1])
        o_t = a[:, :tq] - lam * a[:, tq:]
        o_ref[0, :, h * V_DIM:(h + 1) * V_DIM] = _sub_layer_norm(o_t.T, g_ref[...], lambda_init).astype(o_ref.dtype)


def _prompt_attention(q_t, k, v_t, lam_qk, subln_g, *, lambda_init, tq):
    batch, _, width, seq = q_t.shape
    assert width == N_HEADS * V_DIM
    return pl.pallas_call(
        functools.partial(_attn_kernel, tq=tq, lambda_init=lambda_init),
        grid=(batch, seq // tq),
        in_specs=[
            pl.BlockSpec((1, 2, width, tq), lambda b, i: (b, 0, 0, i)),
            pl.BlockSpec((1, seq, width), lambda b, i: (b, 0, 0)),
            pl.BlockSpec((1, width, seq), lambda b, i: (b, 0, 0)),
            pl.BlockSpec(lam_qk.shape, lambda b, i: (0, 0)),
            pl.BlockSpec(subln_g.shape, lambda b, i: (0, 0)),
        ],
        out_specs=pl.BlockSpec((1, tq, width), lambda b, i: (b, i, 0)),
        out_shape=jax.ShapeDtypeStruct((batch, seq, width), BF16),
        scratch_shapes=[
            pltpu.VMEM((N_HEADS, tq, 2 * tq), F32),
            pltpu.VMEM((N_HEADS, tq, 2 * tq), F32),
            pltpu.VMEM((SUBLANES, 2 * tq), F32),
            pltpu.VMEM((SUBLANES, 2 * tq), F32),
            pltpu.VMEM((N_HEADS, V_DIM, 2 * tq), F32),
        ],
        compiler_params=_compiler_params(2),
        name="prompt_attention",
    )(q_t, k, v_t, lam_qk, subln_g)


def _decode_kernel(pt_ref, q_ref, *refs, pages_per_step, page, n_new, lambda_init):
    del pt_ref
    npg = pages_per_step
    p = pl.program_id(1)
    _decode_step(q_ref, refs[:npg], refs[npg:2 * npg], *refs[2 * npg:], is_first=p == 0,
                 is_last=p == pl.num_programs(1) - 1, page=page, n_new=n_new, lambda_init=lambda_init)


def _decode_step(q_ref, k_refs, v_refs, knew_ref, vnew_ref, lam_ref, g_ref, o_ref, kbf_scr, vbf_scr, m_scr, l_scr,
                 acc_scr, *, is_first, is_last, page, n_new, lambda_init, alongside=()):
    npg = len(k_refs)
    rows = q_ref.shape[2]
    alongside = list(alongside) + [None] * (4 - len(alongside))

    def emit(slot):
        if alongside[slot] is not None:
            alongside[slot]()

    @pl.when(is_first)
    def _():
        m_scr[...] = jnp.full(m_scr.shape, -jnp.inf, F32)
        l_scr[...] = jnp.zeros(l_scr.shape, F32)
        acc_scr[...] = jnp.zeros(acc_scr.shape, F32)

    def softmax_part(k_t, mask=None):
        s = jnp.concatenate(
            [jnp.dot(q_ref[0, h].astype(BF16), k_t(h), preferred_element_type=F32) for h in range(N_HEADS)], axis=0)
        if mask is not None:
            s = jnp.where(mask, s, -jnp.inf)
        n_rep = s.shape[-1] // LANES
        m_prev = m_scr[...]
        m_next = jnp.maximum(m_prev, jnp.max(s, axis=-1, keepdims=True))
        pr = jnp.exp(s - jnp.concatenate([m_next] * n_rep, axis=-1))
        alpha = jnp.exp(m_prev - m_next)
        m_scr[...] = m_next
        l_scr[...] = alpha * l_scr[...] + jnp.sum(pr, axis=-1, keepdims=True)
        return pr, alpha

    def value_part(pr, alpha, v):
        pv = jnp.concatenate(
            [jnp.dot(pr[h * rows:(h + 1) * rows].astype(BF16), v(h), preferred_element_type=F32)
             for h in range(N_HEADS)], axis=0)
        acc_scr[...] = alpha * acc_scr[...] + pv

    def update(k_t, v, mask=None):
        value_part(*softmax_part(k_t, mask), v)

    emit(0)
    for i in range(npg):
        kbf_scr[:, i * page:(i + 1) * page] = k_refs[i][0, 0].astype(BF16)
        for h in range(N_HEADS):
            vbf_scr[h, i * page:(i + 1) * page, :] = (
                v_refs[i][0, 0, pl.ds(h, page, stride=N_HEADS), :].astype(BF16))
    emit(1)
    pr, alpha = softmax_part(lambda h: kbf_scr[h * V_DIM:(h + 1) * V_DIM, :])
    emit(2)
    value_part(pr, alpha, lambda h: vbf_scr[h])
    emit(3)

    @pl.when(is_last)
    def _():
        row_t = lax.broadcasted_iota(jnp.int32, (N_HEADS * rows, LANES), 0) % n_new
        col = lax.broadcasted_iota(jnp.int32, (N_HEADS * rows, LANES), 1)
        update(lambda h: knew_ref[0, h * V_DIM:(h + 1) * V_DIM, :].astype(BF16),
               lambda h: vnew_ref[0, :, h * V_DIM:(h + 1) * V_DIM].astype(BF16), col <= row_t)
        lam = _lambda_value(lam_ref, lambda_init)
        a = acc_scr[...] / l_scr[...]
        for h in range(N_HEADS):
            ah = a[h * rows:(h + 1) * rows]
            o = ah - lam * pltpu.roll(ah, rows - n_new, 0)
            o_ref[0, h] = _sub_layer_norm(o, g_ref[...], lambda_init)


def _decode_attention(qd, cache_k, cache_v, page_table, k_new, v_new, lam_qk, subln_g, *, layer, lambda_init,
                      pages_per_step):
    n_seq, _, rows, _ = qd.shape
    page = cache_k.shape[3]
    npg = pages_per_step
    in_specs, out_spec, scratch = _decode_specs(qd, cache_k, page_table, lam_qk, subln_g, layer=layer, npg=npg,
                                                locate=lambda n, p: (n, p))
    grid_spec = pltpu.PrefetchScalarGridSpec(
        num_scalar_prefetch=1,
        grid=(n_seq, page_table.shape[1] // npg),
        in_specs=in_specs,
        out_specs=out_spec,
        scratch_shapes=scratch,
    )
    return pl.pallas_call(
        functools.partial(_decode_kernel, pages_per_step=npg, page=page, n_new=rows // 2, lambda_init=lambda_init),
        grid_spec=grid_spec,
        out_shape=jax.ShapeDtypeStruct(qd.shape, F32),
        compiler_params=_compiler_params(2),
        name="decode_attention",
    )(page_table, qd, *([cache_k] * npg), *([cache_v] * npg), k_new, v_new, lam_qk, subln_g)


def _decode_specs(qd, cache_k, page_table, lam_qk, subln_g, *, layer, npg, locate):
    _, n_heads, rows, _ = qd.shape
    _, _, width, page = cache_k.shape
    assert page_table.shape[1] % npg == 0 and n_heads == N_HEADS and rows % SUBLANES == 0 and page == LANES

    def per_seq(*tail):
        return lambda *ids: (locate(*ids[:-1])[0],) + tail

    def page_map(i):
        def index_map(*ids):
            n, p = locate(*ids[:-1])
            return (layer, ids[-1][n, p * npg + i], 0, 0)
        return index_map

    q_spec = pl.BlockSpec((1, n_heads, rows, V_DIM), per_seq(0, 0, 0))
    in_specs = (
        [q_spec]
        + [pl.BlockSpec((1, 1, width, page), page_map(i)) for i in range(npg)]
        + [pl.BlockSpec((1, 1, page * n_heads, V_DIM), page_map(i)) for i in range(npg)]
        + [pl.BlockSpec((1, width, LANES), per_seq(0, 0)),
           pl.BlockSpec((1, LANES, width), per_seq(0, 0)),
           pl.BlockSpec(lam_qk.shape, lambda *ids: (0, 0)),
           pl.BlockSpec(subln_g.shape, lambda *ids: (0, 0))])
    scratch = ([pltpu.VMEM((width, npg * page), BF16), pltpu.VMEM((n_heads, npg * page, V_DIM), BF16)]
               + [pltpu.VMEM((n_heads * rows, LANES), F32)] * 3)
    return in_specs, q_spec, scratch


FFN_SLICES = 4


def _ffn_decode_kernel(pt_ref, x_ref, ada_ref, wup_ref, wdn_ref, lng_ref, lnb_ref, q_ref, *refs,
                       sub, alpha, d_ff, ck, groups_per_seq, pages_per_step, page, n_new, lambda_init):
    del pt_ref
    npg = pages_per_step
    k_refs, v_refs = refs[:npg], refs[npg:2 * npg]
    decode_refs = refs[2 * npg:2 * npg + 4]
    out_ref, od_ref, h_scr, f_scr = refs[2 * npg + 4:2 * npg + 8]
    decode_scr = refs[2 * npg + 8:]
    j = pl.program_id(2)
    step = (pl.program_id(0) * pl.num_programs(1) + pl.program_id(1)) * FFN_SLICES + j
    group = step % groups_per_seq
    n_chunks = d_ff // ck
    per_slice = -(-n_chunks // FFN_SLICES)
    bounds = [min(n_chunks, t * per_slice) for t in range(FFN_SLICES + 1)]
    shift, scale, gate = _ada_vectors(ada_ref, sub, False)

    for t in range(FFN_SLICES):

        @pl.when(j == t)
        def _(t=t):
            hid = []

            def chunk(c, first=False):
                def emit():
                    if first and t == 0:
                        h_scr[...] = (x_ref[0] * (1.0 + scale) + shift).astype(BF16)
                    h = h_scr[...]
                    g = jnp.dot(h, wup_ref[:, c * ck:(c + 1) * ck], preferred_element_type=F32)
                    u = jnp.dot(h, wup_ref[:, d_ff + c * ck:d_ff + (c + 1) * ck], preferred_element_type=F32)
                    hid.append((g * jax.nn.sigmoid(g) * u).astype(BF16))
                return emit

            def finish():
                f_part = jnp.dot(jnp.concatenate(hid, axis=-1), wdn_ref[bounds[t] * ck:bounds[t + 1] * ck, :],
                                 preferred_element_type=F32)
                if t == 0:
                    f_scr[...] = f_part
                elif t < FFN_SLICES - 1:
                    f_scr[...] += f_part
                else:
                    y = alpha * x_ref[0] + (1.0 + gate) * (0.5 * (f_scr[...] + f_part))
                    out_ref[0] = _layer_norm(y, lng_ref[...], lnb_ref[...])

            chunks = [chunk(c, first=c == bounds[t]) for c in range(bounds[t], bounds[t + 1])]
            pieces = [chunks[0], chunks[1] if len(chunks) > 1 else None,
                      (lambda: [emit() for emit in chunks[2:]]), finish]
            _decode_step(q_ref, k_refs, v_refs, *decode_refs, od_ref, *decode_scr, is_first=group == 0,
                         is_last=group == groups_per_seq - 1, page=page, n_new=n_new, lambda_init=lambda_init,
                         alongside=pieces)


def _ffn_decode(x, ada, w_up, w_dn, ln_g, ln_b, qd, cache_k, cache_v, page_table, k_new, v_new, lam_qk, subln_g, *,
                layer, which, sub, alpha, tm, lambda_init, pages_per_step):
    batch, seq, d_model = x.shape
    d_ff = w_dn.shape[2]
    ck = 256 if d_ff % 256 == 0 else d_ff
    n_seq, _, rows, _ = qd.shape
    page = cache_k.shape[3]
    npg = pages_per_step
    tiles = seq // tm
    groups_per_seq = page_table.shape[1] // npg
    assert batch * tiles * FFN_SLICES == n_seq * groups_per_seq and d_ff // ck >= FFN_SLICES

    def locate(b, i, j):
        step = (b * tiles + i) * FFN_SLICES + j
        return step // groups_per_seq, step % groups_per_seq

    dec_in, dec_out, dec_scratch = _decode_specs(qd, cache_k, page_table, lam_qk, subln_g, layer=layer, npg=npg,
                                                 locate=locate)
    row_spec = pl.BlockSpec((1, tm, d_model), lambda b, i, j, pt: (b, i, 0))
    grid_spec = pltpu.PrefetchScalarGridSpec(
        num_scalar_prefetch=1,
        grid=(batch, tiles, FFN_SLICES),
        in_specs=[
            row_spec,
            pl.BlockSpec((1, 3 * N_SUB, d_model), lambda b, i, j, pt: (b, 0, 0)),
            _const_spec((None, None, d_model, 2 * d_ff), lambda b, i, j, pt: (layer, which, 0, 0)),
            _const_spec((None, None, d_ff, d_model), lambda b, i, j, pt: (layer, which, 0, 0)),
            _const_spec(ln_g.shape, lambda b, i, j, pt: (0, 0)),
            _const_spec(ln_b.shape, lambda b, i, j, pt: (0, 0)),
        ] + dec_in,
        out_specs=[row_spec, dec_out],
        scratch_shapes=[pltpu.VMEM((tm, d_model), BF16), pltpu.VMEM((tm, d_model), F32)] + dec_scratch,
    )
    return pl.pallas_call(
        functools.partial(_ffn_decode_kernel, sub=sub, alpha=alpha, d_ff=d_ff, ck=ck, groups_per_seq=groups_per_seq,
                          pages_per_step=npg, page=page, n_new=rows // 2, lambda_init=lambda_init),
        grid_spec=grid_spec,
        out_shape=[jax.ShapeDtypeStruct(x.shape, F32), jax.ShapeDtypeStruct(qd.shape, F32)],
        compiler_params=_compiler_params(3),
        name="ffn_decode",
    )(page_table, x, ada, w_up, w_dn, ln_g, ln_b, qd, *([cache_k] * npg), *([cache_v] * npg), k_new, v_new,
      lam_qk, subln_g)


def _pick_tile(rows, target):
    tile = min(rows, target)
    while rows % tile:
        tile //= 2
    return tile


def kernel(x_prompt, x_sample, cache_k, cache_v, state_conv, page_table, c_prompt, c_sample, w_ada, b_ada, ln_g, ln_b, ffn_w_up, ffn_w_down, w_in, w_conv, lambda_qk, subln_g, w_o):
    depth = w_ada.shape[0]
    batch, seq, d_model = x_prompt.shape
    n_seq, n_new, _ = x_sample.shape
    _, n_pool, page, n_heads, _, head_dim = cache_k.shape
    assert n_heads == N_HEADS and head_dim == HEAD_DIM and page == LANES
    cw = w_conv.shape[-1]
    qk_w = N_HEADS * V_DIM
    past_len = page_table.shape[1] * page
    alpha = (2.0 * depth) ** 0.25
    s_rows = n_seq * n_new

    w_up_b = ffn_w_up.astype(BF16)
    w_dn_b = ffn_w_down.astype(BF16)
    w_in_b = w_in.astype(BF16)
    w_o_b = w_o.astype(BF16)
    cache_k2 = jnp.transpose(cache_k, (0, 1, 3, 4, 5, 2)).reshape(depth, n_pool, qk_w, page)
    cache_v2 = cache_v.reshape(depth, n_pool, page * N_HEADS, V_DIM)

    ada_all = _adaln(jnp.concatenate([c_prompt, c_sample], axis=0), w_ada, b_ada)
    ada_all = ada_all.reshape(depth, batch + n_seq, 3 * N_SUB, d_model)

    tab_p = _rope_tables(jnp.arange(seq, dtype=F32))
    tab_s = _rope_tables(past_len + (jnp.arange(s_rows, dtype=jnp.int32) % n_new).astype(F32))

    tm_p = _pick_tile(seq, 512)
    tq = _pick_tile(seq, 256)
    pages_per_step = _pick_tile(page_table.shape[1], 16)
    fused = batch * (seq // tm_p) * FFN_SLICES == n_seq * (page_table.shape[1] // pages_per_step)

    xp = x_prompt
    xs = x_sample.reshape(1, s_rows, d_model)
    k_stack = v_stack = None
    outs = {k: [] for k in ("cp", "ks", "vs", "cs")}
    for l in range(depth):
        lambda_init = 0.8 - 0.6 * math.exp(-0.3 * l)
        lam_l = lambda_qk[l]
        g_l = subln_g[l].reshape(1, V_DIM)
        ln_g0, ln_b0 = ln_g[l, 0:1], ln_b[l, 0:1]
        ln_g12, ln_b12 = ln_g[l, 1:3], ln_b[l, 1:3]

        ada_p = ada_all[l, :batch]
        ada_s = jnp.repeat(jnp.transpose(ada_all[l, batch:], (1, 0, 2)), n_new, axis=1)
        pre = state_conv[l]
        zeros = jnp.zeros((n_seq, n_new - 1, cw), F32)
        p1 = jnp.concatenate([pre[:, 1:2], zeros], axis=1).reshape(1, s_rows, cw)
        p2 = jnp.concatenate([pre, zeros[:, 1:]], axis=1).reshape(1, s_rows, cw)
        xs = _ffn(xs, ada_s, w_up_b, w_dn_b, ln_g0, ln_b0, layer=l, which=0, sub=0, per_row=True,
                  alpha=alpha, tm=s_rows)
        yc, qz, kf, vf, u_s = _mixer_sample(xs, ada_s, w_in_b, w_conv, tab_s, (p1, p2), layer=l, rows_per_seq=n_new)
        qd = qz.astype(F32).reshape(2, n_seq, n_new, N_HEADS, V_DIM)
        qd = jnp.transpose(qd, (1, 3, 0, 2, 4)).reshape(n_seq, N_HEADS, 2 * n_new, V_DIM)
        pad = ((0, 0), (0, LANES - n_new), (0, 0))
        k_new = jnp.transpose(jnp.pad(kf.reshape(n_seq, n_new, qk_w), pad), (0, 2, 1))
        v_new = jnp.pad(vf.reshape(n_seq, n_new, qk_w), pad)

        if fused:
            xp, od = _ffn_decode(xp, ada_p, w_up_b, w_dn_b, ln_g0, ln_b0, qd, cache_k2, cache_v2, page_table,
                                 k_new, v_new, lam_l, g_l, layer=l, which=0, sub=0, alpha=alpha, tm=tm_p,
                                 lambda_init=lambda_init, pages_per_step=pages_per_step)
        else:
            xp = _ffn(xp, ada_p, w_up_b, w_dn_b, ln_g0, ln_b0, layer=l, which=0, sub=0, per_row=False,
                      alpha=alpha, tm=tm_p)
            od = _decode_attention(qd, cache_k2, cache_v2, page_table, k_new, v_new, lam_l, g_l, layer=l,
                                   lambda_init=lambda_init, pages_per_step=pages_per_step)

        yc_p, q_t, kb, v_t, conv, k_stack, v_stack = _mixer_prompt(
            xp, ada_p, w_in_b, w_conv, tab_p, k_stack, v_stack, layer=l, depth=depth, tm=tm_p)
        o_p = _prompt_attention(q_t, kb, v_t, lam_l, g_l, lambda_init=lambda_init, tq=tq)
        xp = _ffn(xp, ada_p, w_up_b, w_dn_b, ln_g12, ln_b12, layer=l, which=1, sub=2, per_row=False,
                  alpha=alpha, tm=tm_p, mix=(yc_p, o_p), w_o=w_o_b)
        outs["cp"].append(conv)

        o = jnp.transpose(od[:, :, :n_new], (0, 2, 1, 3)).reshape(1, s_rows, qk_w).astype(BF16)
        xs = _ffn(xs, ada_s, w_up_b, w_dn_b, ln_g12, ln_b12, layer=l, which=1, sub=2, per_row=True,
                  alpha=alpha, tm=s_rows, mix=(yc, o), w_o=w_o_b)
        outs["ks"].append(kf.reshape(n_seq, n_new, N_HEADS, 2, HEAD_DIM))
        outs["vs"].append(vf.reshape(n_seq, n_new, N_HEADS, V_DIM))
        outs["cs"].append(u_s.reshape(n_seq, n_new, cw)[:, n_new - (CONV_K - 1):])

    k_prompt = jnp.transpose(k_stack.reshape(depth, batch, N_HEADS, 2, HEAD_DIM, seq), (0, 1, 5, 2, 3, 4))
    v_prompt = v_stack.reshape(depth, batch, seq, N_HEADS, V_DIM)
    return (xp, xs.reshape(n_seq, n_new, d_model), k_prompt, v_prompt, jnp.stack(outs["cp"]),
            jnp.stack(outs["ks"]), jnp.stack(outs["vs"]), jnp.stack(outs["cs"]))
```

```python
import functools
import math

import jax
import jax.numpy as jnp
from jax import lax
from jax.experimental import pallas as pl
from jax.experimental.pallas import tpu as pltpu

N_HEADS = 4
HEAD_DIM = 64
V_DIM = 2 * HEAD_DIM
CONV_K = 3
ROT_DIM = HEAD_DIM // 4
ROPE_THETA = 500000.0
LN_EPS = 1e-5
RMS_EPS = 1e-5
N_SUB = 3
LOG2_E = math.log2(math.e)

LANES = 128
SUBLANES = 8
V7X_VMEM_BYTES = 64 * 1024 * 1024
VMEM_LIMIT_BYTES = V7X_VMEM_BYTES - 8 * 1024 * 1024

F32 = jnp.float32
BF16 = jnp.bfloat16


def _compiler_params(n_grid_dims):
    return pltpu.CompilerParams(
        dimension_semantics=("arbitrary",) * n_grid_dims,
        vmem_limit_bytes=VMEM_LIMIT_BYTES,
    )


def _const_spec(shape, index_map):
    return pl.BlockSpec(shape, index_map, pipeline_mode=pl.Buffered(1))


def _layer_norm(y, g, b):
    mu = jnp.mean(y, axis=-1, keepdims=True)
    d = y - mu
    var = jnp.mean(d * d, axis=-1, keepdims=True)
    return d * lax.rsqrt(var + LN_EPS) * g + b


def _ada_vectors(ada_ref, sub, per_row):
    if per_row:
        return ada_ref[3 * sub], ada_ref[3 * sub + 1], ada_ref[3 * sub + 2]
    a = ada_ref[0]
    return a[3 * sub:3 * sub + 1], a[3 * sub + 1:3 * sub + 2], a[3 * sub + 2:3 * sub + 3]


def _ada_spec(per_row, tm, d_model):
    if per_row:
        return pl.BlockSpec((3 * N_SUB, tm, d_model), lambda g, j: (0, j, 0))
    return pl.BlockSpec((1, 3 * N_SUB, d_model), lambda g, j: (g, 0, 0))


def _adaln_kernel(c_ref, w_ref, b_ref, o_ref):
    c = c_ref[...]
    h = (c * jax.nn.sigmoid(c)).astype(BF16)
    o_ref[0] = jnp.dot(h, w_ref[0].astype(BF16), preferred_element_type=F32) + b_ref[0]


def _adaln(c_all, w_ada, b_ada):
    depth, d_model, n_out = w_ada.shape
    rows = c_all.shape[0]
    tn = 1024 if n_out % 1024 == 0 else n_out
    return pl.pallas_call(
        _adaln_kernel,
        grid=(depth, n_out // tn),
        in_specs=[
            pl.BlockSpec((rows, d_model), lambda l, n: (0, 0)),
            pl.BlockSpec((1, d_model, tn), lambda l, n: (l, 0, n)),
            pl.BlockSpec((1, 1, tn), lambda l, n: (l, 0, n)),
        ],
        out_specs=pl.BlockSpec((1, rows, tn), lambda l, n: (l, 0, n)),
        out_shape=jax.ShapeDtypeStruct((depth, rows, n_out), F32),
        compiler_params=_compiler_params(2),
        name="adaln",
    )(c_all, w_ada, b_ada.reshape(depth, 1, n_out))


def _rope_table_kernel(pos_ref, inv_ref, cos_ref, sa_ref, sb_ref):
    ang = pos_ref[...] * inv_ref[...]
    lane = lax.broadcasted_iota(jnp.int32, ang.shape, 1) % HEAD_DIM
    half = ROT_DIM // 2
    c = jnp.cos(ang)
    s = jnp.sin(ang)
    cos_ref[...] = jnp.where(lane < ROT_DIM, c, 1.0)
    sa_ref[...] = jnp.where(lane < half, -s, 0.0)
    sb_ref[...] = jnp.where((lane >= half) & (lane < ROT_DIM), s, 0.0)


def _rope_tables(pos):
    rows = pos.shape[0]
    half = ROT_DIM // 2
    lane = jnp.arange(LANES, dtype=jnp.int32) % HEAD_DIM % half
    inv_freq = jnp.power(ROPE_THETA, -(2 * lane).astype(F32) / ROT_DIM).reshape(1, LANES)
    shp = jax.ShapeDtypeStruct((rows, LANES), F32)
    return pl.pallas_call(
        _rope_table_kernel,
        out_shape=(shp, shp, shp),
        name="rope_tables",
    )(pos.reshape(rows, 1), inv_freq)


def _ffn_kernel(*refs, sub, per_row, alpha, d_ff, ck, with_outproj):
    if with_outproj:
        (x_ref, yc_ref, o_ref_in, ada_ref, wo_ref, wup_ref, wdn_ref, lng_ref, lnb_ref,
         out_ref, hid_scr) = refs
    else:
        x_ref, ada_ref, wup_ref, wdn_ref, lng_ref, lnb_ref, out_ref, hid_scr = refs
    x = x_ref[0]
    if with_outproj:
        _, _, gate1 = _ada_vectors(ada_ref, 1, per_row)
        mix = jnp.concatenate([yc_ref[0], o_ref_in[0]], axis=-1)
        y = jnp.dot(mix, wo_ref[...], preferred_element_type=F32)
        x = _layer_norm(alpha * x + (1.0 + gate1) * y, lng_ref[0:1], lnb_ref[0:1])
        ln_row = 1
    else:
        ln_row = 0
    shift, scale, gate = _ada_vectors(ada_ref, sub, per_row)
    h = (x * (1.0 + scale) + shift).astype(BF16)
    for j in range(d_ff // ck):
        g = jnp.dot(h, wup_ref[:, j * ck:(j + 1) * ck], preferred_element_type=F32)
        u = jnp.dot(h, wup_ref[:, d_ff + j * ck:d_ff + (j + 1) * ck], preferred_element_type=F32)
        hid_scr[:, j * ck:(j + 1) * ck] = (g * jax.nn.sigmoid(g) * u).astype(BF16)
    f = jnp.dot(hid_scr[...], wdn_ref[...], preferred_element_type=F32)
    y = alpha * x + (1.0 + gate) * (0.5 * f)
    out_ref[0] = _layer_norm(y, lng_ref[ln_row:ln_row + 1], lnb_ref[ln_row:ln_row + 1])


def _ffn(x, ada, w_up, w_dn, ln_g, ln_b, *, layer, which, sub, per_row, alpha, tm, mix=None, w_o=None):
    n_groups, rows, d_model = x.shape
    d_ff = w_dn.shape[2]
    ck = 256 if d_ff % 256 == 0 else d_ff
    with_outproj = mix is not None
    row_spec = pl.BlockSpec((1, tm, d_model), lambda g, j: (g, j, 0))
    in_specs = [row_spec]
    args = [x]
    if with_outproj:
        width = mix[0].shape[-1]
        mix_spec = pl.BlockSpec((1, tm, width), lambda g, j: (g, j, 0))
        in_specs += [mix_spec, mix_spec]
        args += list(mix)
    in_specs.append(_ada_spec(per_row, tm, d_model))
    args.append(ada)
    if with_outproj:
        in_specs.append(_const_spec((None, w_o.shape[1], d_model), lambda g, j: (layer, 0, 0)))
        args.append(w_o)
    in_specs += [
        _const_spec((None, None, d_model, 2 * d_ff), lambda g, j: (layer, which, 0, 0)),
        _const_spec((None, None, d_ff, d_model), lambda g, j: (layer, which, 0, 0)),
        _const_spec(ln_g.shape, lambda g, j: (0, 0)),
        _const_spec(ln_b.shape, lambda g, j: (0, 0)),
    ]
    args += [w_up, w_dn, ln_g, ln_b]
    return pl.pallas_call(
        functools.partial(_ffn_kernel, sub=sub, per_row=per_row, alpha=alpha, d_ff=d_ff, ck=ck,
                          with_outproj=with_outproj),
        grid=(n_groups, rows // tm),
        in_specs=in_specs,
        out_specs=row_spec,
        out_shape=jax.ShapeDtypeStruct(x.shape, F32),
        scratch_shapes=[pltpu.VMEM((tm, d_ff), BF16)],
        compiler_params=_compiler_params(2),
        name="ffn_outproj" if with_outproj else "ffn",
    )(*args)


def _rope(z, cos, sa, sb):
    half = ROT_DIM // 2
    outs = []
    for s in range(z.shape[-1] // LANES):
        zs = z[:, s * LANES:(s + 1) * LANES]
        up = pltpu.roll(zs, LANES - half, 1)
        dn = pltpu.roll(zs, half, 1)
        outs.append(zs * cos + up * sa + dn * sb)
    return jnp.concatenate(outs, axis=-1)


def _mixer_core(x_ref, ada_ref, win_ref, wconv_ref, cos_ref, sa_ref, sb_ref, uext_scr, prefix_refs, *,
                per_row, rows_per_seq, tm, cw, qk_w):
    j = pl.program_id(1)
    x = x_ref[0]
    shift, scale, _ = _ada_vectors(ada_ref, 1, per_row)
    h = (x * (1.0 + scale) + shift).astype(BF16)

    def proj(lo, width):
        return jnp.dot(h, win_ref[:, lo:lo + width], preferred_element_type=F32)

    gb = proj(0, cw)
    gc = proj(cw, cw)
    xc = proj(2 * cw, cw)
    u = gc * xc

    @pl.when(j == 0)
    def _():
        uext_scr[0:SUBLANES, :] = jnp.zeros((SUBLANES, cw), F32)

    @pl.when(j > 0)
    def _():
        uext_scr[0:SUBLANES, :] = uext_scr[tm:tm + SUBLANES, :]

    uext_scr[SUBLANES:SUBLANES + tm, :] = u
    um2 = uext_scr[SUBLANES - 2:SUBLANES - 2 + tm, :]
    um1 = uext_scr[SUBLANES - 1:SUBLANES - 1 + tm, :]
    if per_row:
        p1_ref, p2_ref = prefix_refs
        t = lax.broadcasted_iota(jnp.int32, (tm, cw), 0) % rows_per_seq
        um2 = jnp.where(t < 2, p2_ref[0], um2)
        um1 = jnp.where(t < 1, p1_ref[0], um1)
    wc = wconv_ref[...]
    conv = um2 * wc[0:1] + um1 * wc[1:2] + u * wc[2:3]
    yc = gb * conv

    cos, sa, sb = cos_ref[...], sa_ref[...], sb_ref[...]
    q = _rope(proj(3 * cw, qk_w), cos, sa, sb) * (HEAD_DIM ** -0.5)
    k = _rope(proj(3 * cw + qk_w, qk_w), cos, sa, sb)
    v = proj(3 * cw + 2 * qk_w, qk_w)
    return yc, q, k, v


def _mixer_prompt_kernel(x_ref, ada_ref, win_ref, wconv_ref, cos_ref, sa_ref, sb_ref,
                         yc_ref, qt_ref, kb_ref, vt_ref, conv_ref, kout_ref, vout_ref, uext_scr, *,
                         tm, cw, qk_w, rows_per_seq):
    yc, q, k, v = _mixer_core(x_ref, ada_ref, win_ref, wconv_ref, cos_ref, sa_ref, sb_ref, uext_scr, None,
                              per_row=False, rows_per_seq=rows_per_seq, tm=tm, cw=cw, qk_w=qk_w)
    yc_ref[0] = yc.astype(BF16)
    conv_ref[0] = uext_scr[SUBLANES + tm - (CONV_K - 1):SUBLANES + tm, :]
    q_t = (q * LOG2_E).T
    first = (lax.broadcasted_iota(jnp.int32, q_t.shape, 0) % V_DIM) < HEAD_DIM
    qt_ref[0, 0] = jnp.where(first, q_t, 0.0).astype(BF16)
    qt_ref[0, 1] = jnp.where(first, 0.0, q_t).astype(BF16)
    kb_ref[0] = k.astype(BF16)
    kout_ref[0, 0] = k.T
    vt_ref[0] = v.T.astype(BF16)
    for h in range(N_HEADS):
        vout_ref[0, 0, pl.ds(h, tm, stride=N_HEADS), :] = v[:, h * V_DIM:(h + 1) * V_DIM]


def _mixer_prompt(x, ada, w_in, w_conv, tables, k_stack, v_stack, *, layer, depth, tm):
    batch, seq, d_model = x.shape
    cw = w_conv.shape[-1]
    n_in = w_in.shape[-1]
    qk_w = (n_in - 3 * cw) // 3
    assert qk_w == N_HEADS * V_DIM and n_in == 3 * cw + 3 * qk_w
    aliased = k_stack is not None
    tab_spec = pl.BlockSpec((tm, LANES), lambda g, j: (j, 0))
    in_specs = [
        pl.BlockSpec((1, tm, d_model), lambda g, j: (g, j, 0)),
        _ada_spec(False, tm, d_model),
        _const_spec((None, d_model, n_in), lambda g, j: (layer, 0, 0)),
        _const_spec((None, CONV_K, cw), lambda g, j: (layer, 0, 0)),
        tab_spec, tab_spec, tab_spec,
    ]
    args = [x, ada, w_in, w_conv, *tables]
    aliases = {}
    if aliased:
        in_specs += [pl.BlockSpec(memory_space=pl.ANY)] * 2
        args += [k_stack, v_stack]
        aliases = {7: 5, 8: 6}
    out_specs = [
        pl.BlockSpec((1, tm, cw), lambda g, j: (g, j, 0)),
        pl.BlockSpec((1, 2, qk_w, tm), lambda g, j: (g, 0, 0, j)),
        pl.BlockSpec((1, tm, qk_w), lambda g, j: (g, j, 0)),
        pl.BlockSpec((1, qk_w, tm), lambda g, j: (g, 0, j)),
        pl.BlockSpec((1, CONV_K - 1, cw), lambda g, j: (g, 0, 0)),
        pl.BlockSpec((1, 1, qk_w, tm), lambda g, j: (layer, g, 0, j)),
        pl.BlockSpec((1, 1, tm * N_HEADS, V_DIM), lambda g, j: (layer, g, j, 0)),
    ]
    out_shape = [
        jax.ShapeDtypeStruct((batch, seq, cw), BF16),
        jax.ShapeDtypeStruct((batch, 2, qk_w, seq), BF16),
        jax.ShapeDtypeStruct((batch, seq, qk_w), BF16),
        jax.ShapeDtypeStruct((batch, qk_w, seq), BF16),
        jax.ShapeDtypeStruct((batch, CONV_K - 1, cw), F32),
        jax.ShapeDtypeStruct((depth, batch, qk_w, seq), F32),
        jax.ShapeDtypeStruct((depth, batch, seq * N_HEADS, V_DIM), F32),
    ]
    kernel_fn = functools.partial(_mixer_prompt_kernel, tm=tm, cw=cw, qk_w=qk_w, rows_per_seq=seq)
    if aliased:
        kernel_fn = _drop_refs(kernel_fn, (7, 8))
    return pl.pallas_call(
        kernel_fn,
        grid=(batch, seq // tm),
        in_specs=in_specs,
        out_specs=out_specs,
        out_shape=out_shape,
        input_output_aliases=aliases,
        scratch_shapes=[pltpu.VMEM((tm + 2 * SUBLANES, cw), F32)],
        compiler_params=_compiler_params(2),
        name="mixer_prompt",
    )(*args)


def _drop_refs(kernel_fn, positions):
    def wrapped(*refs):
        return kernel_fn(*[r for i, r in enumerate(refs) if i not in positions])
    return wrapped


def _mixer_sample_kernel(x_ref, ada_ref, win_ref, wconv_ref, cos_ref, sa_ref, sb_ref, p1_ref, p2_ref,
                         yc_ref, qz_ref, kf_ref, vf_ref, u_ref, uext_scr, *, tm, cw, qk_w, rows_per_seq):
    yc, q, k, v = _mixer_core(x_ref, ada_ref, win_ref, wconv_ref, cos_ref, sa_ref, sb_ref, uext_scr,
                              (p1_ref, p2_ref), per_row=True, rows_per_seq=rows_per_seq, tm=tm, cw=cw, qk_w=qk_w)
    yc_ref[0] = yc.astype(BF16)
    u_ref[0] = uext_scr[SUBLANES:SUBLANES + tm, :]
    first = (lax.broadcasted_iota(jnp.int32, q.shape, 1) % V_DIM) < HEAD_DIM
    qz_ref[0, 0] = jnp.where(first, q, 0.0).astype(BF16)
    qz_ref[0, 1] = jnp.where(first, 0.0, q).astype(BF16)
    kf_ref[0] = k
    vf_ref[0] = v


def _mixer_sample(x, ada, w_in, w_conv, tables, prefix, *, layer, rows_per_seq):
    _, rows, d_model = x.shape
    cw = w_conv.shape[-1]
    n_in = w_in.shape[-1]
    qk_w = (n_in - 3 * cw) // 3
    tab_spec = pl.BlockSpec((rows, LANES), lambda g, j: (0, 0))
    w_spec = pl.BlockSpec((1, rows, cw), lambda g, j: (0, 0, 0))
    qk_spec = pl.BlockSpec((1, rows, qk_w), lambda g, j: (0, 0, 0))
    return pl.pallas_call(
        functools.partial(_mixer_sample_kernel, tm=rows, cw=cw, qk_w=qk_w, rows_per_seq=rows_per_seq),
        grid=(1, 1),
        in_specs=[
            pl.BlockSpec((1, rows, d_model), lambda g, j: (0, 0, 0)),
            _ada_spec(True, rows, d_model),
            _const_spec((None, d_model, n_in), lambda g, j: (layer, 0, 0)),
            _const_spec((None, CONV_K, cw), lambda g, j: (layer, 0, 0)),
            tab_spec, tab_spec, tab_spec, w_spec, w_spec,
        ],
        out_specs=[w_spec, pl.BlockSpec((1, 2, rows, qk_w), lambda g, j: (0, 0, 0, 0)), qk_spec, qk_spec, w_spec],
        out_shape=[
            jax.ShapeDtypeStruct((1, rows, cw), BF16),
            jax.ShapeDtypeStruct((1, 2, rows, qk_w), BF16),
            jax.ShapeDtypeStruct((1, rows, qk_w), F32),
            jax.ShapeDtypeStruct((1, rows, qk_w), F32),
            jax.ShapeDtypeStruct((1, rows, cw), F32),
        ],
        scratch_shapes=[pltpu.VMEM((rows + 2 * SUBLANES, cw), F32)],
        compiler_params=_compiler_params(2),
        name="mixer_sample",
    )(x, ada, w_in, w_conv, *tables, *prefix)


def _lambda_value(lam_ref, lambda_init):
    lf = lam_ref[...]
    a = jnp.sum(lf[0:1] * lf[1:2], axis=-1, keepdims=True)
    b = jnp.sum(lf[2:3] * lf[3:4], axis=-1, keepdims=True)
    return jnp.exp(a) - jnp.exp(b) + lambda_init


def _sub_layer_norm(o, g, lambda_init):
    o = o * lax.rsqrt(jnp.mean(o * o, axis=-1, keepdims=True) + RMS_EPS) * g
    return o * (1.0 - lambda_init)


def _attn_kernel(qt_ref, k_ref, vt_ref, lam_ref, g_ref, o_ref, sa_scr, sb_scr, m_scr, l_scr, acc_scr, *,
                 tq, lambda_init):
    qi = pl.program_id(1)
    m_scr[...] = jnp.full(m_scr.shape, -jnp.inf, F32)
    l_scr[...] = jnp.zeros(l_scr.shape, F32)
    acc_scr[...] = jnp.zeros(acc_scr.shape, F32)

    def scores(j, s_scr):
        start = pl.multiple_of(j * tq, tq)
        for h in range(N_HEADS):
            hs = slice(h * V_DIM, (h + 1) * V_DIM)
            q_t = jnp.concatenate([qt_ref[0, 0, hs, :], qt_ref[0, 1, hs, :]], axis=-1)
            s_scr[h] = jnp.dot(k_ref[0, pl.ds(start, tq), hs], q_t, preferred_element_type=F32)

    def consume(j, s_scr, masked):
        start = pl.multiple_of(j * tq, tq)
        for h in range(N_HEADS):
            s = s_scr[h]
            if masked:
                key = lax.broadcasted_iota(jnp.int32, s.shape, 0)
                qry = lax.broadcasted_iota(jnp.int32, s.shape, 1) % tq
                s = jnp.where(key <= qry, s, -jnp.inf)
            m_prev = m_scr[h:h + 1]
            m_next = jnp.maximum(m_prev, jnp.max(s, axis=0, keepdims=True))
            p = jnp.exp2(s - m_next)
            alpha = jnp.exp2(m_prev - m_next)
            m_scr[h:h + 1] = m_next
            l_scr[h:h + 1] = alpha * l_scr[h:h + 1] + jnp.sum(p, axis=0, keepdims=True)
            v_t = vt_ref[0, h * V_DIM:(h + 1) * V_DIM, pl.ds(start, tq)]
            acc_scr[h] = alpha * acc_scr[h] + jnp.dot(v_t, p.astype(BF16), preferred_element_type=F32)

    scores(0, sa_scr)

    def pair(jj, carry):
        scores(2 * jj + 1, sb_scr)
        consume(2 * jj, sa_scr, False)
        scores(2 * jj + 2, sa_scr)
        consume(2 * jj + 1, sb_scr, False)
        return carry

    lax.fori_loop(0, qi // 2, pair, 0)

    @pl.when(qi % 2 == 0)
    def _():
        consume(qi, sa_scr, True)

    @pl.when(qi % 2 == 1)
    def _():
        scores(qi, sb_scr)
        consume(qi - 1, sa_scr, False)
        consume(qi, sb_scr, True)

    lam = _lambda_value(lam_ref, lambda_init)
    for h in range(N_HEADS):
        a = acc_scr[h] * (1.0 / l_scr[h:h + 1])
        o_t = a[:, :tq] - lam * a[:, tq:]
        o_ref[0, :, h * V_DIM:(h + 1) * V_DIM] = _sub_layer_norm(o_t.T, g_ref[...], lambda_init).astype(o_ref.dtype)


def _prompt_attention(q_t, k, v_t, lam_qk, subln_g, *, lambda_init, tq):
    batch, _, width, seq = q_t.shape
    assert width == N_HEADS * V_DIM
    return pl.pallas_call(
        functools.partial(_attn_kernel, tq=tq, lambda_init=lambda_init),
        grid=(batch, seq // tq),
        in_specs=[
            pl.BlockSpec((1, 2, width, tq), lambda b, i: (b, 0, 0, i)),
            pl.BlockSpec((1, seq, width), lambda b, i: (b, 0, 0)),
            pl.BlockSpec((1, width, seq), lambda b, i: (b, 0, 0)),
            pl.BlockSpec(lam_qk.shape, lambda b, i: (0, 0)),
            pl.BlockSpec(subln_g.shape, lambda b, i: (0, 0)),
        ],
        out_specs=pl.BlockSpec((1, tq, width), lambda b, i: (b, i, 0)),
        out_shape=jax.ShapeDtypeStruct((batch, seq, width), BF16),
        scratch_shapes=[
            pltpu.VMEM((N_HEADS, tq, 2 * tq), F32),
            pltpu.VMEM((N_HEADS, tq, 2 * tq), F32),
            pltpu.VMEM((SUBLANES, 2 * tq), F32),
            pltpu.VMEM((SUBLANES, 2 * tq), F32),
            pltpu.VMEM((N_HEADS, V_DIM, 2 * tq), F32),
        ],
        compiler_params=_compiler_params(2),
        name="prompt_attention",
    )(q_t, k, v_t, lam_qk, subln_g)


def _decode_kernel(pt_ref, q_ref, *refs, pages_per_step, page, n_new, lambda_init):
    del pt_ref
    npg = pages_per_step
    p = pl.program_id(1)
    _decode_step(q_ref, refs[:npg], refs[npg:2 * npg], *refs[2 * npg:], is_first=p == 0,
                 is_last=p == pl.num_programs(1) - 1, page=page, n_new=n_new, lambda_init=lambda_init)


def _decode_step(q_ref, k_refs, v_refs, knew_ref, vnew_ref, lam_ref, g_ref, o_ref, kbf_scr, vbf_scr, m_scr, l_scr,
                 acc_scr, *, is_first, is_last, page, n_new, lambda_init, alongside=()):
    npg = len(k_refs)
    rows = q_ref.shape[2]
    alongside = list(alongside) + [None] * (4 - len(alongside))

    def emit(slot):
        if alongside[slot] is not None:
            alongside[slot]()

    @pl.when(is_first)
    def _():
        m_scr[...] = jnp.full(m_scr.shape, -jnp.inf, F32)
        l_scr[...] = jnp.zeros(l_scr.shape, F32)
        acc_scr[...] = jnp.zeros(acc_scr.shape, F32)

    def softmax_part(k_t, mask=None):
        s = jnp.concatenate(
            [jnp.dot(q_ref[0, h].astype(BF16), k_t(h), preferred_element_type=F32) for h in range(N_HEADS)], axis=0)
        if mask is not None:
            s = jnp.where(mask, s, -jnp.inf)
        n_rep = s.shape[-1] // LANES
        m_prev = m_scr[...]
        m_next = jnp.maximum(m_prev, jnp.max(s, axis=-1, keepdims=True))
        pr = jnp.exp(s - jnp.concatenate([m_next] * n_rep, axis=-1))
        alpha = jnp.exp(m_prev - m_next)
        m_scr[...] = m_next
        l_scr[...] = alpha * l_scr[...] + jnp.sum(pr, axis=-1, keepdims=True)
        return pr, alpha

    def value_part(pr, alpha, v):
        pv = jnp.concatenate(
            [jnp.dot(pr[h * rows:(h + 1) * rows].astype(BF16), v(h), preferred_element_type=F32)
             for h in range(N_HEADS)], axis=0)
        acc_scr[...] = alpha * acc_scr[...] + pv

    def update(k_t, v, mask=None):
        value_part(*softmax_part(k_t, mask), v)

    emit(0)
    for i in range(npg):
        kbf_scr[:, i * page:(i + 1) * page] = k_refs[i][0, 0].astype(BF16)
        for h in range(N_HEADS):
            vbf_scr[h, i * page:(i + 1) * page, :] = (
                v_refs[i][0, 0, pl.ds(h, page, stride=N_HEADS), :].astype(BF16))
    emit(1)
    pr, alpha = softmax_part(lambda h: kbf_scr[h * V_DIM:(h + 1) * V_DIM, :])
    emit(2)
    value_part(pr, alpha, lambda h: vbf_scr[h])
    emit(3)

    @pl.when(is_last)
    def _():
        row_t = lax.broadcasted_iota(jnp.int32, (N_HEADS * rows, LANES), 0) % n_new
        col = lax.broadcasted_iota(jnp.int32, (N_HEADS * rows, LANES), 1)
        update(lambda h: knew_ref[0, h * V_DIM:(h + 1) * V_DIM, :].astype(BF16),
               lambda h: vnew_ref[0, :, h * V_DIM:(h + 1) * V_DIM].astype(BF16), col <= row_t)
        lam = _lambda_value(lam_ref, lambda_init)
        a = acc_scr[...] / l_scr[...]
        for h in range(N_HEADS):
            ah = a[h * rows:(h + 1) * rows]
            o = ah - lam * pltpu.roll(ah, rows - n_new, 0)
            o_ref[0, h] = _sub_layer_norm(o, g_ref[...], lambda_init)


def _decode_attention(qd, cache_k, cache_v, page_table, k_new, v_new, lam_qk, subln_g, *, layer, lambda_init,
                      pages_per_step):
    n_seq, _, rows, _ = qd.shape
    page = cache_k.shape[3]
    npg = pages_per_step
    in_specs, out_spec, scratch = _decode_specs(qd, cache_k, page_table, lam_qk, subln_g, layer=layer, npg=npg,
                                                locate=lambda n, p: (n, p))
    grid_spec = pltpu.PrefetchScalarGridSpec(
        num_scalar_prefetch=1,
        grid=(n_seq, page_table.shape[1] // npg),
        in_specs=in_specs,
        out_specs=out_spec,
        scratch_shapes=scratch,
    )
    return pl.pallas_call(
        functools.partial(_decode_kernel, pages_per_step=npg, page=page, n_new=rows // 2, lambda_init=lambda_init),
        grid_spec=grid_spec,
        out_shape=jax.ShapeDtypeStruct(qd.shape, F32),
        compiler_params=_compiler_params(2),
        name="decode_attention",
    )(page_table, qd, *([cache_k] * npg), *([cache_v] * npg), k_new, v_new, lam_qk, subln_g)


def _decode_specs(qd, cache_k, page_table, lam_qk, subln_g, *, layer, npg, locate):
    _, n_heads, rows, _ = qd.shape
    _, _, width, page = cache_k.shape
    assert page_table.shape[1] % npg == 0 and n_heads == N_HEADS and rows % SUBLANES == 0 and page == LANES

    def per_seq(*tail):
        return lambda *ids: (locate(*ids[:-1])[0],) + tail

    def page_map(i):
        def index_map(*ids):
            n, p = locate(*ids[:-1])
            return (layer, ids[-1][n, p * npg + i], 0, 0)
        return index_map

    q_spec = pl.BlockSpec((1, n_heads, rows, V_DIM), per_seq(0, 0, 0))
    in_specs = (
        [q_spec]
        + [pl.BlockSpec((1, 1, width, page), page_map(i)) for i in range(npg)]
        + [pl.BlockSpec((1, 1, page * n_heads, V_DIM), page_map(i)) for i in range(npg)]
        + [pl.BlockSpec((1, width, LANES), per_seq(0, 0)),
           pl.BlockSpec((1, LANES, width), per_seq(0, 0)),
           pl.BlockSpec(lam_qk.shape, lambda *ids: (0, 0)),
           pl.BlockSpec(subln_g.shape, lambda *ids: (0, 0))])
    scratch = ([pltpu.VMEM((width, npg * page), BF16), pltpu.VMEM((n_heads, npg * page, V_DIM), BF16)]
               + [pltpu.VMEM((n_heads * rows, LANES), F32)] * 3)
    return in_specs, q_spec, scratch


FFN_SLICES = 4


def _ffn_decode_kernel(pt_ref, x_ref, ada_ref, wup_ref, wdn_ref, lng_ref, lnb_ref, q_ref, *refs,
                       sub, alpha, d_ff, ck, groups_per_seq, pages_per_step, page, n_new, lambda_init):
    del pt_ref
    npg = pages_per_step
    k_refs, v_refs = refs[:npg], refs[npg:2 * npg]
    decode_refs = refs[2 * npg:2 * npg + 4]
    out_ref, od_ref, h_scr, f_scr = refs[2 * npg + 4:2 * npg + 8]
    decode_scr = refs[2 * npg + 8:]
    j = pl.program_id(2)
    step = (pl.program_id(0) * pl.num_programs(1) + pl.program_id(1)) * FFN_SLICES + j
    group = step % groups_per_seq
    n_chunks = d_ff // ck
    per_slice = -(-n_chunks // FFN_SLICES)
    bounds = [min(n_chunks, t * per_slice) for t in range(FFN_SLICES + 1)]
    shift, scale, gate = _ada_vectors(ada_ref, sub, False)

    for t in range(FFN_SLICES):

        @pl.when(j == t)
        def _(t=t):
            hid = []

            def chunk(c, first=False):
                def emit():
                    if first and t == 0:
                        h_scr[...] = (x_ref[0] * (1.0 + scale) + shift).astype(BF16)
                    h = h_scr[...]
                    g = jnp.dot(h, wup_ref[:, c * ck:(c + 1) * ck], preferred_element_type=F32)
                    u = jnp.dot(h, wup_ref[:, d_ff + c * ck:d_ff + (c + 1) * ck], preferred_element_type=F32)
                    hid.append((g * jax.nn.sigmoid(g) * u).astype(BF16))
                return emit

            def finish():
                f_part = jnp.dot(jnp.concatenate(hid, axis=-1), wdn_ref[bounds[t] * ck:bounds[t + 1] * ck, :],
                                 preferred_element_type=F32)
                if t == 0:
                    f_scr[...] = f_part
                elif t < FFN_SLICES - 1:
                    f_scr[...] += f_part
                else:
                    y = alpha * x_ref[0] + (1.0 + gate) * (0.5 * (f_scr[...] + f_part))
                    out_ref[0] = _layer_norm(y, lng_ref[...], lnb_ref[...])

            chunks = [chunk(c, first=c == bounds[t]) for c in range(bounds[t], bounds[t + 1])]
            pieces = [chunks[0], chunks[1] if len(chunks) > 1 else None,
                      (lambda: [emit() for emit in chunks[2:]]), finish]
            _decode_step(q_ref, k_refs, v_refs, *decode_refs, od_ref, *decode_scr, is_first=group == 0,
                         is_last=group == groups_per_seq - 1, page=page, n_new=n_new, lambda_init=lambda_init,
                         alongside=pieces)


def _ffn_decode(x, ada, w_up, w_dn, ln_g, ln_b, qd, cache_k, cache_v, page_table, k_new, v_new, lam_qk, subln_g, *,
                layer, which, sub, alpha, tm, lambda_init, pages_per_step):
    batch, seq, d_model = x.shape
    d_ff = w_dn.shape[2]
    ck = 256 if d_ff % 256 == 0 else d_ff
    n_seq, _, rows, _ = qd.shape
    page = cache_k.shape[3]
    npg = pages_per_step
    tiles = seq // tm
    groups_per_seq = page_table.shape[1] // npg
    assert batch * tiles * FFN_SLICES == n_seq * groups_per_seq and d_ff // ck >= FFN_SLICES

    def locate(b, i, j):
        step = (b * tiles + i) * FFN_SLICES + j
        return step // groups_per_seq, step % groups_per_seq

    dec_in, dec_out, dec_scratch = _decode_specs(qd, cache_k, page_table, lam_qk, subln_g, layer=layer, npg=npg,
                                                 locate=locate)
    row_spec = pl.BlockSpec((1, tm, d_model), lambda b, i, j, pt: (b, i, 0))
    grid_spec = pltpu.PrefetchScalarGridSpec(
        num_scalar_prefetch=1,
        grid=(batch, tiles, FFN_SLICES),
        in_specs=[
            row_spec,
            pl.BlockSpec((1, 3 * N_SUB, d_model), lambda b, i, j, pt: (b, 0, 0)),
            _const_spec((None, None, d_model, 2 * d_ff), lambda b, i, j, pt: (layer, which, 0, 0)),
            _const_spec((None, None, d_ff, d_model), lambda b, i, j, pt: (layer, which, 0, 0)),
            _const_spec(ln_g.shape, lambda b, i, j, pt: (0, 0)),
            _const_spec(ln_b.shape, lambda b, i, j, pt: (0, 0)),
        ] + dec_in,
        out_specs=[row_spec, dec_out],
        scratch_shapes=[pltpu.VMEM((tm, d_model), BF16), pltpu.VMEM((tm, d_model), F32)] + dec_scratch,
    )
    return pl.pallas_call(
        functools.partial(_ffn_decode_kernel, sub=sub, alpha=alpha, d_ff=d_ff, ck=ck, groups_per_seq=groups_per_seq,
                          pages_per_step=npg, page=page, n_new=rows // 2, lambda_init=lambda_init),
        grid_spec=grid_spec,
        out_shape=[jax.ShapeDtypeStruct(x.shape, F32), jax.ShapeDtypeStruct(qd.shape, F32)],
        compiler_params=_compiler_params(3),
        name="ffn_decode",
    )(page_table, x, ada, w_up, w_dn, ln_g, ln_b, qd, *([cache_k] * npg), *([cache_v] * npg), k_new, v_new,
      lam_qk, subln_g)


def _pick_tile(rows, target):
    tile = min(rows, target)
    while rows % tile:
        tile //= 2
    return tile


def kernel(x_prompt, x_sample, cache_k, cache_v, state_conv, page_table, c_prompt, c_sample, w_ada, b_ada, ln_g, ln_b, ffn_w_up, ffn_w_down, w_in, w_conv, lambda_qk, subln_g, w_o):
    depth = w_ada.shape[0]
    batch, seq, d_model = x_prompt.shape
    n_seq, n_new, _ = x_sample.shape
    _, n_pool, page, n_heads, _, head_dim = cache_k.shape
    assert n_heads == N_HEADS and head_dim == HEAD_DIM and page == LANES
    cw = w_conv.shape[-1]
    qk_w = N_HEADS * V_DIM
    past_len = page_table.shape[1] * page
    alpha = (2.0 * depth) ** 0.25
    s_rows = n_seq * n_new

    w_up_b = ffn_w_up.astype(BF16)
    w_dn_b = ffn_w_down.astype(BF16)
    w_in_b = w_in.astype(BF16)
    w_o_b = w_o.astype(BF16)
    cache_k2 = jnp.transpose(cache_k, (0, 1, 3, 4, 5, 2)).reshape(depth, n_pool, qk_w, page)
    cache_v2 = cache_v.reshape(depth, n_pool, page * N_HEADS, V_DIM)

    ada_all = _adaln(jnp.concatenate([c_prompt, c_sample], axis=0), w_ada, b_ada)
    ada_all = ada_all.reshape(depth, batch + n_seq, 3 * N_SUB, d_model)

    tab_p = _rope_tables(jnp.arange(seq, dtype=F32))
    tab_s = _rope_tables(past_len + (jnp.arange(s_rows, dtype=jnp.int32) % n_new).astype(F32))

    tm_p = _pick_tile(seq, 512)
    tq = _pick_tile(seq, 256)
    pages_per_step = _pick_tile(page_table.shape[1], 16)
    fused = batch * (seq // tm_p) * FFN_SLICES == n_seq * (page_table.shape[1] // pages_per_step)

    xp = x_prompt
    xs = x_sample.reshape(1, s_rows, d_model)
    k_stack = v_stack = None
    outs = {k: [] for k in ("cp", "ks", "vs", "cs")}
    for l in range(depth):
        lambda_init = 0.8 - 0.6 * math.exp(-0.3 * l)
        lam_l = lambda_qk[l]
        g_l = subln_g[l].reshape(1, V_DIM)
        ln_g0, ln_b0 = ln_g[l, 0:1], ln_b[l, 0:1]
        ln_g12, ln_b12 = ln_g[l, 1:3], ln_b[l, 1:3]

        ada_p = ada_all[l, :batch]
        ada_s = jnp.repeat(jnp.transpose(ada_all[l, batch:], (1, 0, 2)), n_new, axis=1)
        pre = state_conv[l]
        zeros = jnp.zeros((n_seq, n_new - 1, cw), F32)
        p1 = jnp.concatenate([pre[:, 1:2], zeros], axis=1).reshape(1, s_rows, cw)
        p2 = jnp.concatenate([pre, zeros[:, 1:]], axis=1).reshape(1, s_rows, cw)
        xs = _ffn(xs, ada_s, w_up_b, w_dn_b, ln_g0, ln_b0, layer=l, which=0, sub=0, per_row=True,
                  alpha=alpha, tm=s_rows)
        yc, qz, kf, vf, u_s = _mixer_sample(xs, ada_s, w_in_b, w_conv, tab_s, (p1, p2), layer=l, rows_per_seq=n_new)
        qd = qz.astype(F32).reshape(2, n_seq, n_new, N_HEADS, V_DIM)
        qd = jnp.transpose(qd, (1, 3, 0, 2, 4)).reshape(n_seq, N_HEADS, 2 * n_new, V_DIM)
        pad = ((0, 0), (0, LANES - n_new), (0, 0))
        k_new = jnp.transpose(jnp.pad(kf.reshape(n_seq, n_new, qk_w), pad), (0, 2, 1))
        v_new = jnp.pad(vf.reshape(n_seq, n_new, qk_w), pad)

        if fused:
            xp, od = _ffn_decode(xp, ada_p, w_up_b, w_dn_b, ln_g0, ln_b0, qd, cache_k2, cache_v2, page_table,
                                 k_new, v_new, lam_l, g_l, layer=l, which=0, sub=0, alpha=alpha, tm=tm_p,
                                 lambda_init=lambda_init, pages_per_step=pages_per_step)
        else:
            xp = _ffn(xp, ada_p, w_up_b, w_dn_b, ln_g0, ln_b0, layer=l, which=0, sub=0, per_row=False,
                      alpha=alpha, tm=tm_p)
            od = _decode_attention(qd, cache_k2, cache_v2, page_table, k_new, v_new, lam_l, g_l, layer=l,
                                   lambda_init=lambda_init, pages_per_step=pages_per_step)

        yc_p, q_t, kb, v_t, conv, k_stack, v_stack = _mixer_prompt(
            xp, ada_p, w_in_b, w_conv, tab_p, k_stack, v_stack, layer=l, depth=depth, tm=tm_p)
        o_p = _prompt_attention(q_t, kb, v_t, lam_l, g_l, lambda_init=lambda_init, tq=tq)
        xp = _ffn(xp, ada_p, w_up_b, w_dn_b, ln_g12, ln_b12, layer=l, which=1, sub=2, per_row=False,
                  alpha=alpha, tm=tm_p, mix=(yc_p, o_p), w_o=w_o_b)
        outs["cp"].append(conv)

        o = jnp.transpose(od[:, :, :n_new], (0, 2, 1, 3)).reshape(1, s_rows, qk_w).astype(BF16)
        xs = _ffn(xs, ada_s, w_up_b, w_dn_b, ln_g12, ln_b12, layer=l, which=1, sub=2, per_row=True,
                  alpha=alpha, tm=s_rows, mix=(yc, o), w_o=w_o_b)
        outs["ks"].append(kf.reshape(n_seq, n_new, N_HEADS, 2, HEAD_DIM))
        outs["vs"].append(vf.reshape(n_seq, n_new, N_HEADS, V_DIM))
        outs["cs"].append(u_s.reshape(n_seq, n_new, cw)[:, n_new - (CONV_K - 1):])

    k_prompt = jnp.transpose(k_stack.reshape(depth, batch, N_HEADS, 2, HEAD_DIM, seq), (0, 1, 5, 2, 3, 4))
    v_prompt = v_stack.reshape(depth, batch, seq, N_HEADS, V_DIM)
    return (xp, xs.reshape(n_seq, n_new, d_model), k_prompt, v_prompt, jnp.stack(outs["cp"]),
            jnp.stack(outs["ks"]), jnp.stack(outs["vs"]), jnp.stack(outs["cs"]))
```

```python
import functools
import math

import jax
import jax.numpy as jnp
from jax import lax
from jax.experimental import pallas as pl
from jax.experimental.pallas import tpu as pltpu

N_HEADS = 4
HEAD_DIM = 64
V_DIM = 2 * HEAD_DIM
CONV_K = 3
ROT_DIM = HEAD_DIM // 4
ROPE_THETA = 500000.0
LN_EPS = 1e-5
RMS_EPS = 1e-5
N_SUB = 3
LOG2_E = math.log2(math.e)

LANES = 128
SUBLANES = 8
V7X_VMEM_BYTES = 64 * 1024 * 1024
VMEM_LIMIT_BYTES = V7X_VMEM_BYTES - 8 * 1024 * 1024

F32 = jnp.float32
BF16 = jnp.bfloat16


def _compiler_params(n_grid_dims):
    return pltpu.CompilerParams(
        dimension_semantics=("arbitrary",) * n_grid_dims,
        vmem_limit_bytes=VMEM_LIMIT_BYTES,
    )


def _const_spec(shape, index_map):
    return pl.BlockSpec(shape, index_map, pipeline_mode=pl.Buffered(1))


def _layer_norm(y, g, b):
    mu = jnp.mean(y, axis=-1, keepdims=True)
    d = y - mu
    var = jnp.mean(d * d, axis=-1, keepdims=True)
    return d * lax.rsqrt(var + LN_EPS) * g + b


def _ada_vectors(ada_ref, sub, per_row):
    if per_row:
        return ada_ref[3 * sub], ada_ref[3 * sub + 1], ada_ref[3 * sub + 2]
    a = ada_ref[0]
    return a[3 * sub:3 * sub + 1], a[3 * sub + 1:3 * sub + 2], a[3 * sub + 2:3 * sub + 3]


def _ada_spec(per_row, tm, d_model):
    if per_row:
        return pl.BlockSpec((3 * N_SUB, tm, d_model), lambda g, j: (0, j, 0))
    return pl.BlockSpec((1, 3 * N_SUB, d_model), lambda g, j: (g, 0, 0))


def _adaln_kernel(c_ref, w_ref, b_ref, o_ref):
    c = c_ref[...]
    h = (c * jax.nn.sigmoid(c)).astype(BF16)
    o_ref[0] = jnp.dot(h, w_ref[0].astype(BF16), preferred_element_type=F32) + b_ref[0]


def _adaln(c_all, w_ada, b_ada):
    depth, d_model, n_out = w_ada.shape
    rows = c_all.shape[0]
    tn = 1024 if n_out % 1024 == 0 else n_out
    return pl.pallas_call(
        _adaln_kernel,
        grid=(depth, n_out // tn),
        in_specs=[
            pl.BlockSpec((rows, d_model), lambda l, n: (0, 0)),
            pl.BlockSpec((1, d_model, tn), lambda l, n: (l, 0, n)),
            pl.BlockSpec((1, 1, tn), lambda l, n: (l, 0, n)),
        ],
        out_specs=pl.BlockSpec((1, rows, tn), lambda l, n: (l, 0, n)),
        out_shape=jax.ShapeDtypeStruct((depth, rows, n_out), F32),
        compiler_params=_compiler_params(2),
        name="adaln",
    )(c_all, w_ada, b_ada.reshape(depth, 1, n_out))


def _rope_table_kernel(pos_ref, inv_ref, cos_ref, sa_ref, sb_ref):
    ang = pos_ref[...] * inv_ref[...]
    lane = lax.broadcasted_iota(jnp.int32, ang.shape, 1) % HEAD_DIM
    half = ROT_DIM // 2
    c = jnp.cos(ang)
    s = jnp.sin(ang)
    cos_ref[...] = jnp.where(lane < ROT_DIM, c, 1.0)
    sa_ref[...] = jnp.where(lane < half, -s, 0.0)
    sb_ref[...] = jnp.where((lane >= half) & (lane < ROT_DIM), s, 0.0)


def _rope_tables(pos):
    rows = pos.shape[0]
    half = ROT_DIM // 2
    lane = jnp.arange(LANES, dtype=jnp.int32) % HEAD_DIM % half
    inv_freq = jnp.power(ROPE_THETA, -(2 * lane).astype(F32) / ROT_DIM).reshape(1, LANES)
    shp = jax.ShapeDtypeStruct((rows, LANES), F32)
    return pl.pallas_call(
        _rope_table_kernel,
        out_shape=(shp, shp, shp),
        name="rope_tables",
    )(pos.reshape(rows, 1), inv_freq)


def _ffn_kernel(*refs, sub, per_row, alpha, d_ff, ck, with_outproj):
    if with_outproj:
        (x_ref, yc_ref, o_ref_in, ada_ref, wo_ref, wup_ref, wdn_ref, lng_ref, lnb_ref,
         out_ref, hid_scr) = refs
    else:
        x_ref, ada_ref, wup_ref, wdn_ref, lng_ref, lnb_ref, out_ref, hid_scr = refs
    x = x_ref[0]
    if with_outproj:
        _, _, gate1 = _ada_vectors(ada_ref, 1, per_row)
        mix = jnp.concatenate([yc_ref[0], o_ref_in[0]], axis=-1)
        y = jnp.dot(mix, wo_ref[...], preferred_element_type=F32)
        x = _layer_norm(alpha * x + (1.0 + gate1) * y, lng_ref[0:1], lnb_ref[0:1])
        ln_row = 1
    else:
        ln_row = 0
    shift, scale, gate = _ada_vectors(ada_ref, sub, per_row)
    h = (x * (1.0 + scale) + shift).astype(BF16)
    for j in range(d_ff // ck):
        g = jnp.dot(h, wup_ref[:, j * ck:(j + 1) * ck], preferred_element_type=F32)
        u = jnp.dot(h, wup_ref[:, d_ff + j * ck:d_ff + (j + 1) * ck], preferred_element_type=F32)
        hid_scr[:, j * ck:(j + 1) * ck] = (g * jax.nn.sigmoid(g) * u).astype(BF16)
    f = jnp.dot(hid_scr[...], wdn_ref[...], preferred_element_type=F32)
    y = alpha * x + (1.0 + gate) * (0.5 * f)
    out_ref[0] = _layer_norm(y, lng_ref[ln_row:ln_row + 1], lnb_ref[ln_row:ln_row + 1])


def _ffn(x, ada, w_up, w_dn, ln_g, ln_b, *, layer, which, sub, per_row, alpha, tm, mix=None, w_o=None):
    n_groups, rows, d_model = x.shape
    d_ff = w_dn.shape[2]
    ck = 256 if d_ff % 256 == 0 else d_ff
    with_outproj = mix is not None
    row_spec = pl.BlockSpec((1, tm, d_model), lambda g, j: (g, j, 0))
    in_specs = [row_spec]
    args = [x]
    if with_outproj:
        width = mix[0].shape[-1]
        mix_spec = pl.BlockSpec((1, tm, width), lambda g, j: (g, j, 0))
        in_specs += [mix_spec, mix_spec]
        args += list(mix)
    in_specs.append(_ada_spec(per_row, tm, d_model))
    args.append(ada)
    if with_outproj:
        in_specs.append(_const_spec((None, w_o.shape[1], d_model), lambda g, j: (layer, 0, 0)))
        args.append(w_o)
    in_specs += [
        _const_spec((None, None, d_model, 2 * d_ff), lambda g, j: (layer, which, 0, 0)),
        _const_spec((None, None, d_ff, d_model), lambda g, j: (layer, which, 0, 0)),
        _const_spec(ln_g.shape, lambda g, j: (0, 0)),
        _const_spec(ln_b.shape, lambda g, j: (0, 0)),
    ]
    args += [w_up, w_dn, ln_g, ln_b]
    return pl.pallas_call(
        functools.partial(_ffn_kernel, sub=sub, per_row=per_row, alpha=alpha, d_ff=d_ff, ck=ck,
                          with_outproj=with_outproj),
        grid=(n_groups, rows // tm),
        in_specs=in_specs,
        out_specs=row_spec,
        out_shape=jax.ShapeDtypeStruct(x.shape, F32),
        scratch_shapes=[pltpu.VMEM((tm, d_ff), BF16)],
        compiler_params=_compiler_params(2),
        name="ffn_outproj" if with_outproj else "ffn",
    )(*args)


def _rope(z, cos, sa, sb):
    half = ROT_DIM // 2
    outs = []
    for s in range(z.shape[-1] // LANES):
        zs = z[:, s * LANES:(s + 1) * LANES]
        up = pltpu.roll(zs, LANES - half, 1)
        dn = pltpu.roll(zs, half, 1)
        outs.append(zs * cos + up * sa + dn * sb)
    return jnp.concatenate(outs, axis=-1)


def _mixer_core(x_ref, ada_ref, win_ref, wconv_ref, cos_ref, sa_ref, sb_ref, uext_scr, prefix_refs, *,
                per_row, rows_per_seq, tm, cw, qk_w):
    j = pl.program_id(1)
    x = x_ref[0]
    shift, scale, _ = _ada_vectors(ada_ref, 1, per_row)
    h = (x * (1.0 + scale) + shift).astype(BF16)

    def proj(lo, width):
        return jnp.dot(h, win_ref[:, lo:lo + width], preferred_element_type=F32)

    gb = proj(0, cw)
    gc = proj(cw, cw)
    xc = proj(2 * cw, cw)
    u = gc * xc

    @pl.when(j == 0)
    def _():
        uext_scr[0:SUBLANES, :] = jnp.zeros((SUBLANES, cw), F32)

    @pl.when(j > 0)
    def _():
        uext_scr[0:SUBLANES, :] = uext_scr[tm:tm + SUBLANES, :]

    uext_scr[SUBLANES:SUBLANES + tm, :] = u
    um2 = uext_scr[SUBLANES - 2:SUBLANES - 2 + tm, :]
    um1 = uext_scr[SUBLANES - 1:SUBLANES - 1 + tm, :]
    if per_row:
        p1_ref, p2_ref = prefix_refs
        t = lax.broadcasted_iota(jnp.int32, (tm, cw), 0) % rows_per_seq
        um2 = jnp.where(t < 2, p2_ref[0], um2)
        um1 = jnp.where(t < 1, p1_ref[0], um1)
    wc = wconv_ref[...]
    conv = um2 * wc[0:1] + um1 * wc[1:2] + u * wc[2:3]
    yc = gb * conv

    cos, sa, sb = cos_ref[...], sa_ref[...], sb_ref[...]
    q = _rope(proj(3 * cw, qk_w), cos, sa, sb) * (HEAD_DIM ** -0.5)
    k = _rope(proj(3 * cw + qk_w, qk_w), cos, sa, sb)
    v = proj(3 * cw + 2 * qk_w, qk_w)
    return yc, q, k, v


def _mixer_prompt_kernel(x_ref, ada_ref, win_ref, wconv_ref, cos_ref, sa_ref, sb_ref,
                         yc_ref, qt_ref, kb_ref, vt_ref, conv_ref, kout_ref, vout_ref, uext_scr, *,
                         tm, cw, qk_w, rows_per_seq):
    yc, q, k, v = _mixer_core(x_ref, ada_ref, win_ref, wconv_ref, cos_ref, sa_ref, sb_ref, uext_scr, None,
                              per_row=False, rows_per_seq=rows_per_seq, tm=tm, cw=cw, qk_w=qk_w)
    yc_ref[0] = yc.astype(BF16)
    conv_ref[0] = uext_scr[SUBLANES + tm - (CONV_K - 1):SUBLANES + tm, :]
    q_t = (q * LOG2_E).T
    first = (lax.broadcasted_iota(jnp.int32, q_t.shape, 0) % V_DIM) < HEAD_DIM
    qt_ref[0, 0] = jnp.where(first, q_t, 0.0).astype(BF16)
    qt_ref[0, 1] = jnp.where(first, 0.0, q_t).astype(BF16)
    kb_ref[0] = k.astype(BF16)
    kout_ref[0, 0] = k.T
    vt_ref[0] = v.T.astype(BF16)
    for h in range(N_HEADS):
        vout_ref[0, 0, pl.ds(h, tm, stride=N_HEADS), :] = v[:, h * V_DIM:(h + 1) * V_DIM]


def _mixer_prompt(x, ada, w_in, w_conv, tables, k_stack, v_stack, *, layer, depth, tm):
    batch, seq, d_model = x.shape
    cw = w_conv.shape[-1]
    n_in = w_in.shape[-1]
    qk_w = (n_in - 3 * cw) // 3
    assert qk_w == N_HEADS * V_DIM and n_in == 3 * cw + 3 * qk_w
    aliased = k_stack is not None
    tab_spec = pl.BlockSpec((tm, LANES), lambda g, j: (j, 0))
    in_specs = [
        pl.BlockSpec((1, tm, d_model), lambda g, j: (g, j, 0)),
        _ada_spec(False, tm, d_model),
        _const_spec((None, d_model, n_in), lambda g, j: (layer, 0, 0)),
        _const_spec((None, CONV_K, cw), lambda g, j: (layer, 0, 0)),
        tab_spec, tab_spec, tab_spec,
    ]
    args = [x, ada, w_in, w_conv, *tables]
    aliases = {}
    if aliased:
        in_specs += [pl.BlockSpec(memory_space=pl.ANY)] * 2
        args += [k_stack, v_stack]
        aliases = {7: 5, 8: 6}
    out_specs = [
        pl.BlockSpec((1, tm, cw), lambda g, j: (g, j, 0)),
        pl.BlockSpec((1, 2, qk_w, tm), lambda g, j: (g, 0, 0, j)),
        pl.BlockSpec((1, tm, qk_w), lambda g, j: (g, j, 0)),
        pl.BlockSpec((1, qk_w, tm), lambda g, j: (g, 0, j)),
        pl.BlockSpec((1, CONV_K - 1, cw), lambda g, j: (g, 0, 0)),
        pl.BlockSpec((1, 1, qk_w, tm), lambda g, j: (layer, g, 0, j)),
        pl.BlockSpec((1, 1, tm * N_HEADS, V_DIM), lambda g, j: (layer, g, j, 0)),
    ]
    out_shape = [
        jax.ShapeDtypeStruct((batch, seq, cw), BF16),
        jax.ShapeDtypeStruct((batch, 2, qk_w, seq), BF16),
        jax.ShapeDtypeStruct((batch, seq, qk_w), BF16),
        jax.ShapeDtypeStruct((batch, qk_w, seq), BF16),
        jax.ShapeDtypeStruct((batch, CONV_K - 1, cw), F32),
        jax.ShapeDtypeStruct((depth, batch, qk_w, seq), F32),
        jax.ShapeDtypeStruct((depth, batch, seq * N_HEADS, V_DIM), F32),
    ]
    kernel_fn = functools.partial(_mixer_prompt_kernel, tm=tm, cw=cw, qk_w=qk_w, rows_per_seq=seq)
    if aliased:
        kernel_fn = _drop_refs(kernel_fn, (7, 8))
    return pl.pallas_call(
        kernel_fn,
        grid=(batch, seq // tm),
        in_specs=in_specs,
        out_specs=out_specs,
        out_shape=out_shape,
        input_output_aliases=aliases,
        scratch_shapes=[pltpu.VMEM((tm + 2 * SUBLANES, cw), F32)],
        compiler_params=_compiler_params(2),
        name="mixer_prompt",
    )(*args)


def _drop_refs(kernel_fn, positions):
    def wrapped(*refs):
        return kernel_fn(*[r for i, r in enumerate(refs) if i not in positions])
    return wrapped


def _mixer_sample_kernel(x_ref, ada_ref, win_ref, wconv_ref, cos_ref, sa_ref, sb_ref, p1_ref, p2_ref,
                         yc_ref, qz_ref, kf_ref, vf_ref, u_ref, uext_scr, *, tm, cw, qk_w, rows_per_seq):
    yc, q, k, v = _mixer_core(x_ref, ada_ref, win_ref, wconv_ref, cos_ref, sa_ref, sb_ref, uext_scr,
                              (p1_ref, p2_ref), per_row=True, rows_per_seq=rows_per_seq, tm=tm, cw=cw, qk_w=qk_w)
    yc_ref[0] = yc.astype(BF16)
    u_ref[0] = uext_scr[SUBLANES:SUBLANES + tm, :]
    first = (lax.broadcasted_iota(jnp.int32, q.shape, 1) % V_DIM) < HEAD_DIM
    qz_ref[0, 0] = jnp.where(first, q, 0.0).astype(BF16)
    qz_ref[0, 1] = jnp.where(first, 0.0, q).astype(BF16)
    kf_ref[0] = k
    vf_ref[0] = v


def _mixer_sample(x, ada, w_in, w_conv, tables, prefix, *, layer, rows_per_seq):
    _, rows, d_model = x.shape
    cw = w_conv.shape[-1]
    n_in = w_in.shape[-1]
    qk_w = (n_in - 3 * cw) // 3
    tab_spec = pl.BlockSpec((rows, LANES), lambda g, j: (0, 0))
    w_spec = pl.BlockSpec((1, rows, cw), lambda g, j: (0, 0, 0))
    qk_spec = pl.BlockSpec((1, rows, qk_w), lambda g, j: (0, 0, 0))
    return pl.pallas_call(
        functools.partial(_mixer_sample_kernel, tm=rows, cw=cw, qk_w=qk_w, rows_per_seq=rows_per_seq),
        grid=(1, 1),
        in_specs=[
            pl.BlockSpec((1, rows, d_model), lambda g, j: (0, 0, 0)),
            _ada_spec(True, rows, d_model),
            _const_spec((None, d_model, n_in), lambda g, j: (layer, 0, 0)),
            _const_spec((None, CONV_K, cw), lambda g, j: (layer, 0, 0)),
            tab_spec, tab_spec, tab_spec, w_spec, w_spec,
        ],
        out_specs=[w_spec, pl.BlockSpec((1, 2, rows, qk_w), lambda g, j: (0, 0, 0, 0)), qk_spec, qk_spec, w_spec],
        out_shape=[
            jax.ShapeDtypeStruct((1, rows, cw), BF16),
            jax.ShapeDtypeStruct((1, 2, rows, qk_w), BF16),
            jax.ShapeDtypeStruct((1, rows, qk_w), F32),
            jax.ShapeDtypeStruct((1, rows, qk_w), F32),
            jax.ShapeDtypeStruct((1, rows, cw), F32),
        ],
        scratch_shapes=[pltpu.VMEM((rows + 2 * SUBLANES, cw), F32)],
        compiler_params=_compiler_params(2),
        name="mixer_sample",
    )(x, ada, w_in, w_conv, *tables, *prefix)


def _lambda_value(lam_ref, lambda_init):
    lf = lam_ref[...]
    a = jnp.sum(lf[0:1] * lf[1:2], axis=-1, keepdims=True)
    b = jnp.sum(lf[2:3] * lf[3:4], axis=-1, keepdims=True)
    return jnp.exp(a) - jnp.exp(b) + lambda_init


def _sub_layer_norm(o, g, lambda_init):
    o = o * lax.rsqrt(jnp.mean(o * o, axis=-1, keepdims=True) + RMS_EPS) * g
    return o * (1.0 - lambda_init)


def _attn_kernel(qt_ref, k_ref, vt_ref, lam_ref, g_ref, o_ref, sa_scr, sb_scr, m_scr, l_scr, acc_scr, *,
                 tq, lambda_init):
    qi = pl.program_id(1)
    m_scr[...] = jnp.full(m_scr.shape, -jnp.inf, F32)
    l_scr[...] = jnp.zeros(l_scr.shape, F32)
    acc_scr[...] = jnp.zeros(acc_scr.shape, F32)

    def scores(j, s_scr):
        start = pl.multiple_of(j * tq, tq)
        for h in range(N_HEADS):
            hs = slice(h * V_DIM, (h + 1) * V_DIM)
            q_t = jnp.concatenate([qt_ref[0, 0, hs, :], qt_ref[0, 1, hs, :]], axis=-1)
            s_scr[h] = jnp.dot(k_ref[0, pl.ds(start, tq), hs], q_t, preferred_element_type=F32)

    def consume(j, s_scr, masked):
        start = pl.multiple_of(j * tq, tq)
        for h in range(N_HEADS):
            s = s_scr[h]
            if masked:
                key = lax.broadcasted_iota(jnp.int32, s.shape, 0)
                qry = lax.broadcasted_iota(jnp.int32, s.shape, 1) % tq
                s = jnp.where(key <= qry, s, -jnp.inf)
            m_prev = m_scr[h:h + 1]
            m_next = jnp.maximum(m_prev, jnp.max(s, axis=0, keepdims=True))
            p = jnp.exp2(s - m_next)
            alpha = jnp.exp2(m_prev - m_next)
            m_scr[h:h + 1] = m_next
            l_scr[h:h + 1] = alpha * l_scr[h:h + 1] + jnp.sum(p, axis=0, keepdims=True)
            v_t = vt_ref[0, h * V_DIM:(h + 1) * V_DIM, pl.ds(start, tq)]
            acc_scr[h] = alpha * acc_scr[h] + jnp.dot(v_t, p.astype(BF16), preferred_element_type=F32)

    scores(0, sa_scr)

    def pair(jj, carry):
        scores(2 * jj + 1, sb_scr)
        consume(2 * jj, sa_scr, False)
        scores(2 * jj + 2, sa_scr)
        consume(2 * jj + 1, sb_scr, False)
        return carry

    lax.fori_loop(0, qi // 2, pair, 0)

    @pl.when(qi % 2 == 0)
    def _():
        consume(qi, sa_scr, True)

    @pl.when(qi % 2 == 1)
    def _():
        scores(qi, sb_scr)
        consume(qi - 1, sa_scr, False)
        consume(qi, sb_scr, True)

    lam = _lambda_value(lam_ref, lambda_init)
    for h in range(N_HEADS):
        a = acc_scr[h] * (1.0 / l_scr[h:h + 1])
        o_t = a[:, :tq] - lam * a[:, tq:]
        o_ref[0, :, h * V_DIM:(h + 1) * V_DIM] = _sub_layer_norm(o_t.T, g_ref[...], lambda_init).astype(o_ref.dtype)


def _prompt_attention(q_t, k, v_t, lam_qk, subln_g, *, lambda_init, tq):
    batch, _, width, seq = q_t.shape
    assert width == N_HEADS * V_DIM
    return pl.pallas_call(
        functools.partial(_attn_kernel, tq=tq, lambda_init=lambda_init),
        grid=(batch, seq // tq),
        in_specs=[
            pl.BlockSpec((1, 2, width, tq), lambda b, i: (b, 0, 0, i)),
            pl.BlockSpec((1, seq, width), lambda b, i: (b, 0, 0)),
            pl.BlockSpec((1, width, seq), lambda b, i: (b, 0, 0)),
            pl.BlockSpec(lam_qk.shape, lambda b, i: (0, 0)),
            pl.BlockSpec(subln_g.shape, lambda b, i: (0, 0)),
        ],
        out_specs=pl.BlockSpec((1, tq, width), lambda b, i: (b, i, 0)),
        out_shape=jax.ShapeDtypeStruct((batch, seq, width), BF16),
        scratch_shapes=[
            pltpu.VMEM((N_HEADS, tq, 2 * tq), F32),
            pltpu.VMEM((N_HEADS, tq, 2 * tq), F32),
            pltpu.VMEM((SUBLANES, 2 * tq), F32),
            pltpu.VMEM((SUBLANES, 2 * tq), F32),
            pltpu.VMEM((N_HEADS, V_DIM, 2 * tq), F32),
        ],
        compiler_params=_compiler_params(2),
        name="prompt_attention",
    )(q_t, k, v_t, lam_qk, subln_g)


def _decode_kernel(pt_ref, q_ref, *refs, pages_per_step, page, n_new, lambda_init):
    del pt_ref
    npg = pages_per_step
    p = pl.program_id(1)
    _decode_step(q_ref, refs[:npg], refs[npg:2 * npg], *refs[2 * npg:], is_first=p == 0,
                 is_last=p == pl.num_programs(1) - 1, page=page, n_new=n_new, lambda_init=lambda_init)


def _decode_step(q_ref, k_refs, v_refs, knew_ref, vnew_ref, lam_ref, g_ref, o_ref, kbf_scr, vbf_scr, m_scr, l_scr,
                 acc_scr, *, is_first, is_last, page, n_new, lambda_init, alongside=()):
    npg = len(k_refs)
    rows = q_ref.shape[2]
    alongside = list(alongside) + [None] * (4 - len(alongside))

    def emit(slot):
        if alongside[slot] is not None:
            alongside[slot]()

    @pl.when(is_first)
    def _():
        m_scr[...] = jnp.full(m_scr.shape, -jnp.inf, F32)
        l_scr[...] = jnp.zeros(l_scr.shape, F32)
        acc_scr[...] = jnp.zeros(acc_scr.shape, F32)

    def softmax_part(k_t, mask=None):
        s = jnp.concatenate(
            [jnp.dot(q_ref[0, h].astype(BF16), k_t(h), preferred_element_type=F32) for h in range(N_HEADS)], axis=0)
        if mask is not None:
            s = jnp.where(mask, s, -jnp.inf)
        n_rep = s.shape[-1] // LANES
        m_prev = m_scr[...]
        m_next = jnp.maximum(m_prev, jnp.max(s, axis=-1, keepdims=True))
        pr = jnp.exp(s - jnp.concatenate([m_next] * n_rep, axis=-1))
        alpha = jnp.exp(m_prev - m_next)
        m_scr[...] = m_next
        l_scr[...] = alpha * l_scr[...] + jnp.sum(pr, axis=-1, keepdims=True)
        return pr, alpha

    def value_part(pr, alpha, v):
        pv = jnp.concatenate(
            [jnp.dot(pr[h * rows:(h + 1) * rows].astype(BF16), v(h), preferred_element_type=F32)
             for h in range(N_HEADS)], axis=0)
        acc_scr[...] = alpha * acc_scr[...] + pv

    def update(k_t, v, mask=None):
        value_part(*softmax_part(k_t, mask), v)

    emit(0)
    for i in range(npg):
        kbf_scr[:, i * page:(i + 1) * page] = k_refs[i][0, 0].astype(BF16)
        for h in range(N_HEADS):
            vbf_scr[h, i * page:(i + 1) * page, :] = (
                v_refs[i][0, 0, pl.ds(h, page, stride=N_HEADS), :].astype(BF16))
    emit(1)
    pr, alpha = softmax_part(lambda h: kbf_scr[h * V_DIM:(h + 1) * V_DIM, :])
    emit(2)
    value_part(pr, alpha, lambda h: vbf_scr[h])
    emit(3)

    @pl.when(is_last)
    def _():
        row_t = lax.broadcasted_iota(jnp.int32, (N_HEADS * rows, LANES), 0) % n_new
        col = lax.broadcasted_iota(jnp.int32, (N_HEADS * rows, LANES), 1)
        update(lambda h: knew_ref[0, h * V_DIM:(h + 1) * V_DIM, :].astype(BF16),
               lambda h: vnew_ref[0, :, h * V_DIM:(h + 1) * V_DIM].astype(BF16), col <= row_t)
        lam = _lambda_value(lam_ref, lambda_init)
        a = acc_scr[...] / l_scr[...]
        for h in range(N_HEADS):
            ah = a[h * rows:(h + 1) * rows]
            o = ah - lam * pltpu.roll(ah, rows - n_new, 0)
            o_ref[0, h] = _sub_layer_norm(o, g_ref[...], lambda_init)


def _decode_attention(qd, cache_k, cache_v, page_table, k_new, v_new, lam_qk, subln_g, *, layer, lambda_init,
                      pages_per_step):
    n_seq, _, rows, _ = qd.shape
    page = cache_k.shape[3]
    npg = pages_per_step
    in_specs, out_spec, scratch = _decode_specs(qd, cache_k, page_table, lam_qk, subln_g, layer=layer, npg=npg,
                                                locate=lambda n, p: (n, p))
    grid_spec = pltpu.PrefetchScalarGridSpec(
        num_scalar_prefetch=1,
        grid=(n_seq, page_table.shape[1] // npg),
        in_specs=in_specs,
        out_specs=out_spec,
        scratch_shapes=scratch,
    )
    return pl.pallas_call(
        functools.partial(_decode_kernel, pages_per_step=npg, page=page, n_new=rows // 2, lambda_init=lambda_init),
        grid_spec=grid_spec,
        out_shape=jax.ShapeDtypeStruct(qd.shape, F32),
        compiler_params=_compiler_params(2),
        name="decode_attention",
    )(page_table, qd, *([cache_k] * npg), *([cache_v] * npg), k_new, v_new, lam_qk, subln_g)


def _decode_specs(qd, cache_k, page_table, lam_qk, subln_g, *, layer, npg, locate):
    _, n_heads, rows, _ = qd.shape
    _, _, width, page = cache_k.shape
    assert page_table.shape[1] % npg == 0 and n_heads == N_HEADS and rows % SUBLANES == 0 and page == LANES

    def per_seq(*tail):
        return lambda *ids: (locate(*ids[:-1])[0],) + tail

    def page_map(i):
        def index_map(*ids):
            n, p = locate(*ids[:-1])
            return (layer, ids[-1][n, p * npg + i], 0, 0)
        return index_map

    q_spec = pl.BlockSpec((1, n_heads, rows, V_DIM), per_seq(0, 0, 0))
    in_specs = (
        [q_spec]
        + [pl.BlockSpec((1, 1, width, page), page_map(i)) for i in range(npg)]
        + [pl.BlockSpec((1, 1, page * n_heads, V_DIM), page_map(i)) for i in range(npg)]
        + [pl.BlockSpec((1, width, LANES), per_seq(0, 0)),
           pl.BlockSpec((1, LANES, width), per_seq(0, 0)),
           pl.BlockSpec(lam_qk.shape, lambda *ids: (0, 0)),
           pl.BlockSpec(subln_g.shape, lambda *ids: (0, 0))])
    scratch = ([pltpu.VMEM((width, npg * page), BF16), pltpu.VMEM((n_heads, npg * page, V_DIM), BF16)]
               + [pltpu.VMEM((n_heads * rows, LANES), F32)] * 3)
    return in_specs, q_spec, scratch


FFN_SLICES = 4


def _ffn_decode_kernel(pt_ref, x_ref, ada_ref, wup_ref, wdn_ref, lng_ref, lnb_ref, q_ref, *refs,
                       sub, alpha, d_ff, ck, groups_per_seq, pages_per_step, page, n_new, lambda_init):
    del pt_ref
    npg = pages_per_step
    k_refs, v_refs = refs[:npg], refs[npg:2 * npg]
    decode_refs = refs[2 * npg:2 * npg + 4]
    out_ref, od_ref, h_scr, f_scr = refs[2 * npg + 4:2 * npg + 8]
    decode_scr = refs[2 * npg + 8:]
    j = pl.program_id(2)
    step = (pl.program_id(0) * pl.num_programs(1) + pl.program_id(1)) * FFN_SLICES + j
    group = step % groups_per_seq
    n_chunks = d_ff // ck
    per_slice = -(-n_chunks // FFN_SLICES)
    bounds = [min(n_chunks, t * per_slice) for t in range(FFN_SLICES + 1)]
    shift, scale, gate = _ada_vectors(ada_ref, sub, False)

    for t in range(FFN_SLICES):

        @pl.when(j == t)
        def _(t=t):
            hid = []

            def chunk(c, first=False):
                def emit():
                    if first and t == 0:
                        h_scr[...] = (x_ref[0] * (1.0 + scale) + shift).astype(BF16)
                    h = h_scr[...]
                    g = jnp.dot(h, wup_ref[:, c * ck:(c + 1) * ck], preferred_element_type=F32)
                    u = jnp.dot(h, wup_ref[:, d_ff + c * ck:d_ff + (c + 1) * ck], preferred_element_type=F32)
                    hid.append((g * jax.nn.sigmoid(g) * u).astype(BF16))
                return emit

            def finish():
                f_part = jnp.dot(jnp.concatenate(hid, axis=-1), wdn_ref[bounds[t] * ck:bounds[t + 1] * ck, :],
                                 preferred_element_type=F32)
                if t == 0:
                    f_scr[...] = f_part
                elif t < FFN_SLICES - 1:
                    f_scr[...] += f_part
                else:
                    y = alpha * x_ref[0] + (1.0 + gate) * (0.5 * (f_scr[...] + f_part))
                    out_ref[0] = _layer_norm(y, lng_ref[...], lnb_ref[...])

            chunks = [chunk(c, first=c == bounds[t]) for c in range(bounds[t], bounds[t + 1])]
            pieces = [chunks[0], chunks[1] if len(chunks) > 1 else None,
                      (lambda: [emit() for emit in chunks[2:]]), finish]
            _decode_step(q_ref, k_refs, v_refs, *decode_refs, od_ref, *decode_scr, is_first=group == 0,
                         is_last=group == groups_per_seq - 1, page=page, n_new=n_new, lambda_init=lambda_init,
                         alongside=pieces)


def _ffn_decode(x, ada, w_up, w_dn, ln_g, ln_b, qd, cache_k, cache_v, page_table, k_new, v_new, lam_qk, subln_g, *,
                layer, which, sub, alpha, tm, lambda_init, pages_per_step):
    batch, seq, d_model = x.shape
    d_ff = w_dn.shape[2]
    ck = 256 if d_ff % 256 == 0 else d_ff
    n_seq, _, rows, _ = qd.shape
    page = cache_k.shape[3]
    npg = pages_per_step
    tiles = seq // tm
    groups_per_seq = page_table.shape[1] // npg
    assert batch * tiles * FFN_SLICES == n_seq * groups_per_seq and d_ff // ck >= FFN_SLICES

    def locate(b, i, j):
        step = (b * tiles + i) * FFN_SLICES + j
        return step // groups_per_seq, step % groups_per_seq

    dec_in, dec_out, dec_scratch = _decode_specs(qd, cache_k, page_table, lam_qk, subln_g, layer=layer, npg=npg,
                                                 locate=locate)
    row_spec = pl.BlockSpec((1, tm, d_model), lambda b, i, j, pt: (b, i, 0))
    grid_spec = pltpu.PrefetchScalarGridSpec(
        num_scalar_prefetch=1,
        grid=(batch, tiles, FFN_SLICES),
        in_specs=[
            row_spec,
            pl.BlockSpec((1, 3 * N_SUB, d_model), lambda b, i, j, pt: (b, 0, 0)),
            _const_spec((None, None, d_model, 2 * d_ff), lambda b, i, j, pt: (layer, which, 0, 0)),
            _const_spec((None, None, d_ff, d_model), lambda b, i, j, pt: (layer, which, 0, 0)),
            _const_spec(ln_g.shape, lambda b, i, j, pt: (0, 0)),
            _const_spec(ln_b.shape, lambda b, i, j, pt: (0, 0)),
        ] + dec_in,
        out_specs=[row_spec, dec_out],
        scratch_shapes=[pltpu.VMEM((tm, d_model), BF16), pltpu.VMEM((tm, d_model), F32)] + dec_scratch,
    )
    return pl.pallas_call(
        functools.partial(_ffn_decode_kernel, sub=sub, alpha=alpha, d_ff=d_ff, ck=ck, groups_per_seq=groups_per_seq,
                          pages_per_step=npg, page=page, n_new=rows // 2, lambda_init=lambda_init),
        grid_spec=grid_spec,
        out_shape=[jax.ShapeDtypeStruct(x.shape, F32), jax.ShapeDtypeStruct(qd.shape, F32)],
        compiler_params=_compiler_params(3),
        name="ffn_decode",
    )(page_table, x, ada, w_up, w_dn, ln_g, ln_b, qd, *([cache_k] * npg), *([cache_v] * npg), k_new, v_new,
      lam_qk, subln_g)


def _pick_tile(rows, target):
    tile = min(rows, target)
    while rows % tile:
        tile //= 2
    return tile


def kernel(x_prompt, x_sample, cache_k, cache_v, state_conv, page_table, c_prompt, c_sample, w_ada, b_ada, ln_g, ln_b, ffn_w_up, ffn_w_down, w_in, w_conv, lambda_qk, subln_g, w_o):
    depth = w_ada.shape[0]
    batch, seq, d_model = x_prompt.shape
    n_seq, n_new, _ = x_sample.shape
    _, n_pool, page, n_heads, _, head_dim = cache_k.shape
    assert n_heads == N_HEADS and head_dim == HEAD_DIM and page == LANES
    cw = w_conv.shape[-1]
    qk_w = N_HEADS * V_DIM
    past_len = page_table.shape[1] * page
    alpha = (2.0 * depth) ** 0.25
    s_rows = n_seq * n_new

    w_up_b = ffn_w_up.astype(BF16)
    w_dn_b = ffn_w_down.astype(BF16)
    w_in_b = w_in.astype(BF16)
    w_o_b = w_o.astype(BF16)
    cache_k2 = jnp.transpose(cache_k, (0, 1, 3, 4, 5, 2)).reshape(depth, n_pool, qk_w, page)
    cache_v2 = cache_v.reshape(depth, n_pool, page * N_HEADS, V_DIM)

    ada_all = _adaln(jnp.concatenate([c_prompt, c_sample], axis=0), w_ada, b_ada)
    ada_all = ada_all.reshape(depth, batch + n_seq, 3 * N_SUB, d_model)

    tab_p = _rope_tables(jnp.arange(seq, dtype=F32))
    tab_s = _rope_tables(past_len + (jnp.arange(s_rows, dtype=jnp.int32) % n_new).astype(F32))

    tm_p = _pick_tile(seq, 512)
    tq = _pick_tile(seq, 256)
    pages_per_step = _pick_tile(page_table.shape[1], 16)
    fused = batch * (seq // tm_p) * FFN_SLICES == n_seq * (page_table.shape[1] // pages_per_step)

    xp = x_prompt
    xs = x_sample.reshape(1, s_rows, d_model)
    k_stack = v_stack = None
    outs = {k: [] for k in ("cp", "ks", "vs", "cs")}
    for l in range(depth):
        lambda_init = 0.8 - 0.6 * math.exp(-0.3 * l)
        lam_l = lambda_qk[l]
        g_l = subln_g[l].reshape(1, V_DIM)
        ln_g0, ln_b0 = ln_g[l, 0:1], ln_b[l, 0:1]
        ln_g12, ln_b12 = ln_g[l, 1:3], ln_b[l, 1:3]

        ada_p = ada_all[l, :batch]
        ada_s = jnp.repeat(jnp.transpose(ada_all[l, batch:], (1, 0, 2)), n_new, axis=1)
        pre = state_conv[l]
        zeros = jnp.zeros((n_seq, n_new - 1, cw), F32)
        p1 = jnp.concatenate([pre[:, 1:2], zeros], axis=1).reshape(1, s_rows, cw)
        p2 = jnp.concatenate([pre, zeros[:, 1:]], axis=1).reshape(1, s_rows, cw)
        xs = _ffn(xs, ada_s, w_up_b, w_dn_b, ln_g0, ln_b0, layer=l, which=0, sub=0, per_row=True,
                  alpha=alpha, tm=s_rows)
        yc, qz, kf, vf, u_s = _mixer_sample(xs, ada_s, w_in_b, w_conv, tab_s, (p1, p2), layer=l, rows_per_seq=n_new)
        qd = qz.astype(F32).reshape(2, n_seq, n_new, N_HEADS, V_DIM)
        qd = jnp.transpose(qd, (1, 3, 0, 2, 4)).reshape(n_seq, N_HEADS, 2 * n_new, V_DIM)
        pad = ((0, 0), (0, LANES - n_new), (0, 0))
        k_new = jnp.transpose(jnp.pad(kf.reshape(n_seq, n_new, qk_w), pad), (0, 2, 1))
        v_new = jnp.pad(vf.reshape(n_seq, n_new, qk_w), pad)

        if fused:
            xp, od = _ffn_decode(xp, ada_p, w_up_b, w_dn_b, ln_g0, ln_b0, qd, cache_k2, cache_v2, page_table,
                                 k_new, v_new, lam_l, g_l, layer=l, which=0, sub=0, alpha=alpha, tm=tm_p,
                                 lambda_init=lambda_init, pages_per_step=pages_per_step)
        else:
            xp = _ffn(xp, ada_p, w_up_b, w_dn_b, ln_g0, ln_b0, layer=l, which=0, sub=0, per_row=False,
                      alpha=alpha, tm=tm_p)
            od = _decode_attention(qd, cache_k2, cache_v2, page_table, k_new, v_new, lam_l, g_l, layer=l,
                                   lambda_init=lambda_init, pages_per_step=pages_per_step)

        yc_p, q_t, kb, v_t, conv, k_stack, v_stack = _mixer_prompt(
            xp, ada_p, w_in_b, w_conv, tab_p, k_stack, v_stack, layer=l, depth=depth, tm=tm_p)
        o_p = _prompt_attention(q_t, kb, v_t, lam_l, g_l, lambda_init=lambda_init, tq=tq)
        xp = _ffn(xp, ada_p, w_up_b, w_dn_b, ln_g12, ln_b12, layer=l, which=1, sub=2, per_row=False,
                  alpha=alpha, tm=_pick_tile(seq, 1024), mix=(yc_p, o_p), w_o=w_o_b)
        outs["cp"].append(conv)

        o = jnp.transpose(od[:, :, :n_new], (0, 2, 1, 3)).reshape(1, s_rows, qk_w).astype(BF16)
        xs = _ffn(xs, ada_s, w_up_b, w_dn_b, ln_g12, ln_b12, layer=l, which=1, sub=2, per_row=True,
                  alpha=alpha, tm=s_rows, mix=(yc, o), w_o=w_o_b)
        outs["ks"].append(kf.reshape(n_seq, n_new, N_HEADS, 2, HEAD_DIM))
        outs["vs"].append(vf.reshape(n_seq, n_new, N_HEADS, V_DIM))
        outs["cs"].append(u_s.reshape(n_seq, n_new, cw)[:, n_new - (CONV_K - 1):])

    k_prompt = jnp.transpose(k_stack.reshape(depth, batch, N_HEADS, 2, HEAD_DIM, seq), (0, 1, 5, 2, 3, 4))
    v_prompt = v_stack.reshape(depth, batch, seq, N_HEADS, V_DIM)
    return (xp, xs.reshape(n_seq, n_new, d_model), k_prompt, v_prompt, jnp.stack(outs["cp"]),
            jnp.stack(outs["ks"]), jnp.stack(outs["vs"]), jnp.stack(outs["cs"]))
```
